```python
import math
import numpy as np
import jax
import jax.numpy as jnp
from jax import lax

D_MODEL = 1024
BATCH = 8
SEQ = 2048
DEPTH = 4

D_MIX = D_MODEL
HEAD_DIM = 64
FOX_WIDTH = D_MIX // 4
DIL_WIDTH = D_MIX // 4
HGRN_WIDTH = D_MIX // 2
FOX_HEADS = FOX_WIDTH // HEAD_DIM
DIL_HEADS = DIL_WIDTH // HEAD_DIM
HGRN_EXPAND = 128
HGRN_HEADS = HGRN_WIDTH // HGRN_EXPAND
HGRN_VDIM = HGRN_WIDTH // HGRN_HEADS
HGRN_FDIM = HGRN_HEADS * HGRN_EXPAND
HGRN_CHUNK = 64
Q_BLOCK = 128
DILATED_PATTERNS = ((128, 1), (512, 4), (2048, 16))
ROPE_DIM = HEAD_DIM // 4
ROPE_THETA = 500000.0
D_FF = ((8 * D_MODEL // 3 + 127) // 128) * 128
EPS = 1e-6
NEG_BIG = -1e30
LB_FLOOR = 1e-30

SPLIT_SIZES = (FOX_WIDTH, FOX_WIDTH, FOX_WIDTH, FOX_HEADS,
               DIL_WIDTH, DIL_WIDTH, DIL_WIDTH,
               HGRN_FDIM, HGRN_FDIM, HGRN_WIDTH, HGRN_WIDTH)
N_IN = sum(SPLIT_SIZES)
SPLIT_POINTS = tuple(int(s) for s in np.cumsum(SPLIT_SIZES)[:-1])

kernel_name = "hybrid_fox_dilated_hgrn2_macaron"


def _rmsnorm(x, g):
    xf = x.astype(jnp.float32)
    y = xf * lax.rsqrt(jnp.mean(xf * xf, axis=-1, keepdims=True) + EPS)
    return (y * g.astype(jnp.float32)).astype(x.dtype)


def _swiglu(h, wg, wu, wd):
    return (jax.nn.silu(h @ wg) * (h @ wu)) @ wd


def _partial_rope(t, positions):
    half = ROPE_DIM // 2
    freqs = ROPE_THETA ** (-jnp.arange(0, ROPE_DIM, 2, dtype=jnp.float32) / ROPE_DIM)
    ang = positions.astype(jnp.float32)[:, :, None] * freqs
    cos = jnp.cos(ang)[:, :, None, :]
    sin = jnp.sin(ang)[:, :, None, :]
    tf = t.astype(jnp.float32)
    x1, x2, rest = tf[..., :half], tf[..., half:ROPE_DIM], tf[..., ROPE_DIM:]
    out = jnp.concatenate([x1 * cos - x2 * sin, x2 * cos + x1 * sin, rest], axis=-1)
    return out.astype(t.dtype)


def _fox_attention(q, k, v, c):
    B, H, T, dh = q.shape
    nb = T // Q_BLOCK
    scale = dh ** -0.5
    qb = q.reshape(B, H, nb, Q_BLOCK, dh).transpose(2, 0, 1, 3, 4)
    cb = c.reshape(B, H, nb, Q_BLOCK).transpose(2, 0, 1, 3)
    kpos = jnp.arange(T)

    def block(args):
        qi, ci, n = args
        s = jnp.einsum('bhqd,bhkd->bhqk', qi, k).astype(jnp.float32) * scale
        s = s + ci[..., None] - c[:, :, None, :]
        qpos = n * Q_BLOCK + jnp.arange(Q_BLOCK)
        mask = kpos[None, :] <= qpos[:, None]
        p = jax.nn.softmax(jnp.where(mask, s, NEG_BIG), axis=-1)
        return jnp.einsum('bhqk,bhkd->bhqd', p.astype(v.dtype), v)

    out = lax.map(block, (qb, cb, jnp.arange(nb)))
    return out.transpose(1, 2, 0, 3, 4).reshape(B, H, T, dh)


def _dilated_branch(q, k, v, window, dilation):
    B, H, T, dh = q.shape
    L = T // dilation
    w = window // dilation
    blk = min(w, L)
    nb = -(-L // blk)
    Lp = nb * blk
    scale = dh ** -0.5

    def to_blocks(t):
        t = t.reshape(B, H, L, dilation, dh).transpose(0, 1, 3, 2, 4)
        t = jnp.pad(t, ((0, 0), (0, 0), (0, 0), (0, Lp - L), (0, 0)))
        return t.reshape(B, H, dilation, nb, blk, dh)

    def with_prev(t):
        prev = jnp.pad(t, ((0, 0), (0, 0), (0, 0), (1, 0), (0, 0), (0, 0)))[:, :, :, :nb]
        return jnp.concatenate([prev, t], axis=4)

    qb = to_blocks(q)
    kc = with_prev(to_blocks(k))
    vc = with_prev(to_blocks(v))
    s = jnp.einsum('bhrnqd,bhrnkd->bhrnqk', qb, kc).astype(jnp.float32) * scale
    i = jnp.arange(blk)[:, None]
    j = jnp.arange(2 * blk)[None, :]
    rel = blk + i - j
    first = (jnp.arange(nb)[:, None, None] == 0) & (j < blk)[None]
    mask = ((rel >= 0) & (rel <= w))[None] & ~first
    s = jnp.where(mask, s, NEG_BIG)
    m = jnp.max(s, axis=-1, keepdims=True)
    p = jnp.exp(s - m)
    den = jnp.sum(p, axis=-1)
    o = jnp.einsum('bhrnqk,bhrnkd->bhrnqd', p, vc.astype(jnp.float32)) / den[..., None]
    lse = m[..., 0] + jnp.log(den)
    o = o.reshape(B, H, dilation, Lp, dh)[:, :, :, :L].transpose(0, 1, 3, 2, 4).reshape(B, H, T, dh)
    lse = lse.reshape(B, H, dilation, Lp)[..., :L].transpose(0, 1, 3, 2).reshape(B, H, T)
    return o, lse


def _dilated_attention(q, k, v):
    outs, lses = [], []
    for window, dilation in DILATED_PATTERNS:
        o, lse = _dilated_branch(q, k, v, window, dilation)
        outs.append(o)
        lses.append(lse)
    wts = jax.nn.softmax(jnp.stack(lses, 0), axis=0)
    return jnp.sum(wts[..., None] * jnp.stack(outs, 0), axis=0)


def _hgrn2(q_raw, f_raw, i_raw, g_raw, lb, norm_w):
    B, T, _ = q_raw.shape
    H, E, V, C = HGRN_HEADS, HGRN_EXPAND, HGRN_VDIM, HGRN_CHUNK
    nc = T // C
    lbf = jnp.clip(lb.astype(jnp.float32), 0.0, 1.0 - 1e-6)
    z = f_raw.astype(jnp.float32)
    log_f = jnp.logaddexp(jnp.log(jnp.maximum(lbf, LB_FLOOR)),
                          jnp.log1p(-lbf) + jax.nn.log_sigmoid(z))
    kk = (1.0 - lbf) * jax.nn.sigmoid(-z)
    qq = jax.nn.silu(q_raw.astype(jnp.float32))
    vv = i_raw.astype(jnp.float32)

    def chunks(t, d):
        return t.reshape(B, nc, C, H, d).transpose(1, 0, 3, 2, 4)

    causal = jnp.arange(C)[:, None] >= jnp.arange(C)[None, :]

    def step(S, inp):
        q, k, v, lf = inp
        b = jnp.cumsum(lf, axis=2)
        inter = jnp.einsum('bhte,bhev->bhtv', q * jnp.exp(b), S)
        diff = b[:, :, :, None, :] - b[:, :, None, :, :]
        D = jnp.exp(jnp.where(causal[None, None, :, :, None], diff, NEG_BIG))
        A = jnp.einsum('bhtse,bhse->bhts', q[:, :, :, None, :] * D, k)
        intra = jnp.einsum('bhts,bhsv->bhtv', A, v)
        b_last = b[:, :, -1, :]
        S_new = jnp.exp(b_last)[..., None] * S + jnp.einsum(
            'bhse,bhsv->bhev', k * jnp.exp(b_last[:, :, None, :] - b), v)
        return S_new, inter + intra

    S0 = jnp.zeros((B, H, E, V), jnp.float32)
    _, o = lax.scan(step, S0, (chunks(qq, E), chunks(kk, E), chunks(vv, V), chunks(log_f, E)))
    o = o.transpose(1, 0, 3, 2, 4).reshape(B, T, H, V)
    o = o * lax.rsqrt(jnp.mean(o * o, axis=-1, keepdims=True) + EPS)
    o = o.reshape(B, T, H * V) * norm_w.astype(jnp.float32)
    return o * jax.nn.sigmoid(g_raw.astype(jnp.float32))


def _mixing(h, positions, w_in, w_out, fox_b, lb, hgrn_norm):
    B, T, _ = h.shape
    proj = h @ w_in
    (fq, fk, fv, ff, dq, dk, dv, hq, hf, hi, hg) = jnp.split(proj, SPLIT_POINTS, axis=-1)

    def heads(t, n):
        return t.reshape(B, T, n, HEAD_DIM)

    log_fg = jax.nn.log_sigmoid((ff + fox_b).astype(jnp.float32))
    c = jnp.cumsum(log_fg, axis=1).transpose(0, 2, 1)
    tr = lambda t: t.transpose(0, 2, 1, 3)
    oa = _fox_attention(tr(heads(fq, FOX_HEADS)), tr(heads(fk, FOX_HEADS)),
                        tr(heads(fv, FOX_HEADS)), c)
    oa = oa.transpose(0, 2, 1, 3).reshape(B, T, FOX_WIDTH)

    qd = _partial_rope(heads(dq, DIL_HEADS), positions)
    kd = _partial_rope(heads(dk, DIL_HEADS), positions)
    ob = _dilated_attention(tr(qd), tr(kd), tr(heads(dv, DIL_HEADS)))
    ob = ob.transpose(0, 2, 1, 3).reshape(B, T, DIL_WIDTH)

    oc = _hgrn2(hq, hf, hi, hg, lb, hgrn_norm)

    o = jnp.concatenate([oa.astype(h.dtype), ob.astype(h.dtype), oc.astype(h.dtype)], axis=-1)
    return o @ w_out


def setup_inputs(seed: int = 0) -> dict:
    key = jax.random.key(seed)
    ks = jax.random.split(key, 20)
    f32 = jnp.float32
    nrm = lambda k, shape, fan: jax.random.normal(k, shape, f32) * fan ** -0.5
    gain = lambda k, shape: 1.0 + 0.02 * jax.random.normal(k, shape, f32)
    x = jax.random.normal(ks[0], (BATCH, SEQ, D_MODEL), f32)
    positions = jnp.broadcast_to(jnp.arange(SEQ, dtype=jnp.int32)[None, :], (BATCH, SEQ))
    return {
        "x": x,
        "positions": positions,
        "ffn1_norm": gain(ks[1], (DEPTH, D_MODEL)),
        "ffn1_w_gate": nrm(ks[2], (DEPTH, D_MODEL, D_FF), D_MODEL),
        "ffn1_w_up": nrm(ks[3], (DEPTH, D_MODEL, D_FF), D_MODEL),
        "ffn1_w_down": nrm(ks[4], (DEPTH, D_FF, D_MODEL), D_FF),
        "mix_norm": gain(ks[5], (DEPTH, D_MODEL)),
        "w_in": nrm(ks[6], (DEPTH, D_MODEL, N_IN), D_MODEL),
        "fox_forget_bias": 2.0 + 0.5 * jax.random.normal(ks[7], (DEPTH, FOX_HEADS), f32),
        "hgrn_lower_bounds": 0.1 * jax.random.normal(ks[8], (DEPTH, HGRN_FDIM), f32),
        "hgrn_out_norm": gain(ks[9], (DEPTH, HGRN_WIDTH)),
        "w_out": nrm(ks[10], (DEPTH, D_MIX, D_MODEL), D_MIX),
        "ffn2_norm": gain(ks[11], (DEPTH, D_MODEL)),
        "ffn2_w_gate": nrm(ks[12], (DEPTH, D_MODEL, D_FF), D_MODEL),
        "ffn2_w_up": nrm(ks[13], (DEPTH, D_MODEL, D_FF), D_MODEL),
        "ffn2_w_down": nrm(ks[14], (DEPTH, D_FF, D_MODEL), D_FF),
        "final_norm": gain(ks[15], (D_MODEL,)),
    }


def reference(x, positions, ffn1_norm, ffn1_w_gate, ffn1_w_up, ffn1_w_down, mix_norm, w_in,
              fox_forget_bias, hgrn_lower_bounds, hgrn_out_norm, w_out, ffn2_norm,
              ffn2_w_gate, ffn2_w_up, ffn2_w_down, final_norm):
    sm = jax.nn.softmax(hgrn_lower_bounds.astype(jnp.float32), axis=0)
    lbs = jnp.cumsum(sm, axis=0) - sm[0:1]
    for i in range(DEPTH):
        h = _rmsnorm(x, ffn1_norm[i])
        x = x + 0.5 * _swiglu(h, ffn1_w_gate[i], ffn1_w_up[i], ffn1_w_down[i])
        h = _rmsnorm(x, mix_norm[i])
        x = x + _mixing(h, positions, w_in[i], w_out[i], fox_forget_bias[i], lbs[i],
                        hgrn_out_norm[i])
        h = _rmsnorm(x, ffn2_norm[i])
        x = x + 0.5 * _swiglu(h, ffn2_w_gate[i], ffn2_w_up[i], ffn2_w_down[i])
    return _rmsnorm(x, final_norm)
```

```python
import functools
import math

import numpy as np
import jax
import jax.numpy as jnp
from jax import lax
from jax.experimental import pallas as pl
from jax.experimental.pallas import tpu as pltpu

F32 = jnp.float32
BF16 = jnp.bfloat16

D_MODEL = 1024
HEAD_DIM = 64
N_HEADS = 4
ATT_W = N_HEADS * HEAD_DIM
HG_HEADS = 4
HG_E = 128
HG_V = 128
HG_W = HG_HEADS * HG_V
ROPE_DIM = HEAD_DIM // 4
ROPE_THETA = 500000.0
D_FF = ((8 * D_MODEL // 3 + 127) // 128) * 128
EPS = 1e-6
NEG_BIG = -1e30
LB_FLOOR = 1e-30
DILATED_PATTERNS = ((128, 1), (512, 4), (2048, 16))
FF_LANES = 128

V7X_VMEM_BYTES = 64 * 1024 * 1024
VMEM_LIMIT = V7X_VMEM_BYTES - 8 * 1024 * 1024

TM_FFN = 512
FF_CHUNK = 512
TM_PROJ = 512
BQ = 256
BK = 256
HG_CHUNK = 64
HG_SUB = 16
HG_TB = 512
CS_BLK = 256


def _rms(x, g):
    ms = jnp.mean(x * x, axis=-1, keepdims=True)
    return x * lax.rsqrt(ms + EPS) * g


def _sigmoid(x):
    return 1.0 / (1.0 + jnp.exp(-x))


def _params(sem):
    return pltpu.CompilerParams(dimension_semantics=sem, vmem_limit_bytes=VMEM_LIMIT)


def _ffn_kernel(final, x_ref, g_ref, wg_ref, wu_ref, wd_ref, fg_ref, o_ref, act_ref):
    x = x_ref[...]
    h = _rms(x, g_ref[...]).astype(BF16)
    for c in range(0, D_FF, FF_CHUNK):
        w = min(FF_CHUNK, D_FF - c)
        g = jnp.dot(h, wg_ref[:, c:c + w], preferred_element_type=F32)
        u = jnp.dot(h, wu_ref[:, c:c + w], preferred_element_type=F32)
        act_ref[:, c:c + w] = (g * _sigmoid(g) * u).astype(BF16)
    y = x + 0.5 * jnp.dot(act_ref[...], wd_ref[...], preferred_element_type=F32)
    if final:
        y = _rms(y, fg_ref[...])
    o_ref[...] = y


def _ffn(x2d, g, wg, wu, wd, fg, final):
    n = x2d.shape[0]
    const = lambda i: (0, 0)
    return pl.pallas_call(
        functools.partial(_ffn_kernel, final),
        grid=(n // TM_FFN,),
        in_specs=[
            pl.BlockSpec((TM_FFN, D_MODEL), lambda i: (i, 0)),
            pl.BlockSpec((1, D_MODEL), const),
            pl.BlockSpec((D_MODEL, D_FF), const, pipeline_mode=pl.Buffered(1)),
            pl.BlockSpec((D_MODEL, D_FF), const, pipeline_mode=pl.Buffered(1)),
            pl.BlockSpec((D_FF, D_MODEL), const, pipeline_mode=pl.Buffered(1)),
            pl.BlockSpec((1, D_MODEL), const),
        ],
        out_specs=pl.BlockSpec((TM_FFN, D_MODEL), lambda i: (i, 0)),
        out_shape=jax.ShapeDtypeStruct((n, D_MODEL), F32),
        scratch_shapes=[pltpu.VMEM((TM_FFN, D_FF), BF16)],
        compiler_params=_params(("parallel",)),
        name="ffn_final" if final else "ffn",
    )(x2d, g, wg, wu, wd, fg)


def _inproj_kernel(x_ref, g_ref, wa_ref, wbt_ref, cos_ref, sin_ref, cost_ref, sint_ref, fb_ref,
                   fk_ref, lf_ref, dk_ref, hq_ref, hf_ref, hi_ref, hg_ref,
                   fqt_ref, fvt_ref, dqt_ref, dvt_ref):
    h = _rms(x_ref[0], g_ref[...]).astype(BF16)
    W = ATT_W
    pa = jnp.dot(h, wa_ref[:, 0:3 * W + FF_LANES], preferred_element_type=F32)
    fk_ref[0] = pa[:, 0:W].astype(BF16)
    dk_ref[0] = (pa[:, W:2 * W] * cos_ref[0] + pa[:, 2 * W:3 * W] * sin_ref[0]).astype(BF16)
    z = pa[:, 3 * W:3 * W + FF_LANES] + fb_ref[...]
    lf_ref[0] = jnp.minimum(z, 0.0) - jnp.log1p(jnp.exp(-jnp.abs(z)))
    base = 3 * W + FF_LANES
    for i, ref in enumerate((hq_ref, hf_ref, hi_ref, hg_ref)):
        ref[0] = jnp.dot(h, wa_ref[:, base + i * HG_W:base + (i + 1) * HG_W],
                         preferred_element_type=F32)
    pb = lax.dot_general(wbt_ref[...], h, (((1,), (1,)), ((), ())),
                         preferred_element_type=F32)
    fqt_ref[0] = pb[0:W].astype(BF16)
    dqt_ref[0] = (pb[2 * W:3 * W] * cost_ref[0] + pb[3 * W:4 * W] * sint_ref[0]).astype(BF16)
    fvt = pb[W:2 * W].astype(BF16)
    dvt = pb[4 * W:5 * W].astype(BF16)
    for j in range(TM_PROJ // BK):
        fvt_ref[0, j] = fvt[:, j * BK:(j + 1) * BK]
        dvt_ref[0, j] = dvt[:, j * BK:(j + 1) * BK]


def _inproj(x, g, wa, wbt, cos, sin, cost, sint, fb):
    B, T, _ = x.shape
    W = ATT_W
    nt = T // TM_PROJ
    const = lambda b, t: (0, 0)
    row = lambda b, t: (b, t, 0)
    col = lambda b, t: (b, 0, t)
    blk4 = lambda b, t: (b, t, 0, 0)
    out_shape = [
        jax.ShapeDtypeStruct((B, T, W), BF16),
        jax.ShapeDtypeStruct((B, T, FF_LANES), F32),
        jax.ShapeDtypeStruct((B, T, W), BF16),
        jax.ShapeDtypeStruct((B, T, HG_W), F32),
        jax.ShapeDtypeStruct((B, T, HG_W), F32),
        jax.ShapeDtypeStruct((B, T, HG_W), F32),
        jax.ShapeDtypeStruct((B, T, HG_W), F32),
        jax.ShapeDtypeStruct((B, W, T), BF16),
        jax.ShapeDtypeStruct((B, T // BK, W, BK), BF16),
        jax.ShapeDtypeStruct((B, W, T), BF16),
        jax.ShapeDtypeStruct((B, T // BK, W, BK), BF16),
    ]
    out_specs = [
        pl.BlockSpec((1, TM_PROJ, W), row),
        pl.BlockSpec((1, TM_PROJ, FF_LANES), row),
        pl.BlockSpec((1, TM_PROJ, W), row),
        pl.BlockSpec((1, TM_PROJ, HG_W), row),
        pl.BlockSpec((1, TM_PROJ, HG_W), row),
        pl.BlockSpec((1, TM_PROJ, HG_W), row),
        pl.BlockSpec((1, TM_PROJ, HG_W), row),
        pl.BlockSpec((1, W, TM_PROJ), col),
        pl.BlockSpec((1, TM_PROJ // BK, W, BK), blk4),
        pl.BlockSpec((1, W, TM_PROJ), col),
        pl.BlockSpec((1, TM_PROJ // BK, W, BK), blk4),
    ]
    return pl.pallas_call(
        _inproj_kernel,
        grid=(B, nt),
        in_specs=[
            pl.BlockSpec((1, TM_PROJ, D_MODEL), row),
            pl.BlockSpec((1, D_MODEL), const),
            pl.BlockSpec(wa.shape, const, pipeline_mode=pl.Buffered(1)),
            pl.BlockSpec(wbt.shape, const, pipeline_mode=pl.Buffered(1)),
            pl.BlockSpec((1, TM_PROJ, W), row),
            pl.BlockSpec((1, TM_PROJ, W), row),
            pl.BlockSpec((1, W, TM_PROJ), col),
            pl.BlockSpec((1, W, TM_PROJ), col),
            pl.BlockSpec((1, FF_LANES), const),
        ],
        out_specs=out_specs,
        out_shape=out_shape,
        compiler_params=_params(("parallel", "parallel")),
        name="inproj",
    )(x, g, wa, wbt, cos, sin, cost, sint, fb)


def _attn_kernel(fox, qt_ref, k_ref, vt_ref, aux_ref, o_ref, *scratch):
    qi = pl.program_id(1)
    T = k_ref.shape[1]

    if fox:
        c_ref, = scratch

        @pl.when(qi == 0)
        def _():
            r = lax.broadcasted_iota(jnp.int32, (CS_BLK, CS_BLK), 0)
            c = lax.broadcasted_iota(jnp.int32, (CS_BLK, CS_BLK), 1)
            tril = (c <= r).astype(F32)
            carry = jnp.zeros((1, FF_LANES), F32)
            for i in range(T // CS_BLK):
                blk = jnp.dot(tril, aux_ref[0, i * CS_BLK:(i + 1) * CS_BLK, :],
                              precision=lax.Precision.HIGHEST,
                              preferred_element_type=F32) + carry
                c_ref[i * CS_BLK:(i + 1) * CS_BLK, :] = blk
                carry = blk[CS_BLK - 1:CS_BLK, :]

    qt = qt_ref[0]
    head_of_row = lax.broadcasted_iota(jnp.int32, qt.shape, 0) // HEAD_DIM
    krow = lax.broadcasted_iota(jnp.int32, (BK, BQ), 0)
    qcol = lax.broadcasted_iota(jnp.int32, (BK, BQ), 1)

    outs = []
    for h in range(N_HEADS):
        qm = jnp.where(head_of_row == h, qt, jnp.zeros_like(qt))

        def step(j, carry, diag=False, h=h, qm=qm):
            m, l, acc = carry
            start = pl.multiple_of(j * BK, BK)
            kb = k_ref[0, pl.ds(start, BK), :]
            st = jnp.dot(kb, qm, preferred_element_type=F32)
            if fox:
                st = st - c_ref[pl.ds(start, BK), :][:, h:h + 1]
                if diag:
                    st = jnp.where(krow <= qcol, st, NEG_BIG)
            else:
                st = st + aux_ref[qi - j]
            m_new = jnp.maximum(m, jnp.max(st, axis=0, keepdims=True))
            p = jnp.exp(st - m_new)
            alpha = jnp.exp(m - m_new)
            l = alpha * l + jnp.sum(p, axis=0, keepdims=True)
            vb = vt_ref[0, j, h * HEAD_DIM:(h + 1) * HEAD_DIM, :]
            acc = alpha * acc + jnp.dot(vb, p.astype(BF16), preferred_element_type=F32)
            return m_new, l, acc

        init = (jnp.full((1, BQ), NEG_BIG, F32), jnp.zeros((1, BQ), F32),
                jnp.zeros((HEAD_DIM, BQ), F32))
        if fox:
            carry = lax.fori_loop(0, qi, step, init)
            m, l, acc = step(qi, carry, diag=True)
        else:
            m, l, acc = lax.fori_loop(0, qi + 1, step, init)
        outs.append(acc / l)
    ot = jnp.concatenate(outs, axis=0)
    o_ref[0] = ot.T.astype(BF16)


def _attention(fox, qt, k, vt, aux):
    B, T, W = k.shape
    nq = T // BQ
    if fox:
        aux_spec = pl.BlockSpec((1, T, FF_LANES), lambda b, q: (b, 0, 0))
        scratch = [pltpu.VMEM((T, FF_LANES), F32)]
    else:
        aux_spec = pl.BlockSpec(aux.shape, lambda b, q: (0, 0, 0))
        scratch = []
    return pl.pallas_call(
        functools.partial(_attn_kernel, fox),
        grid=(B, nq),
        in_specs=[
            pl.BlockSpec((1, W, BQ), lambda b, q: (b, 0, q)),
            pl.BlockSpec((1, T, W), lambda b, q: (b, 0, 0)),
            pl.BlockSpec((1, T // BK, W, BK), lambda b, q: (b, 0, 0, 0)),
            aux_spec,
        ],
        out_specs=pl.BlockSpec((1, BQ, W), lambda b, q: (b, q, 0)),
        out_shape=jax.ShapeDtypeStruct((B, T, W), BF16),
        scratch_shapes=scratch,
        compiler_params=_params(("parallel", "arbitrary")),
        name="fox_attention" if fox else "dilated_attention",
    )(qt, k, vt, aux)


def _dilated_bias_table():
    nd = max(w for w, _ in DILATED_PATTERNS) // BQ
    d = (np.arange(nd)[:, None, None] * BQ + np.arange(BQ)[None, None, :]
         - np.arange(BK)[None, :, None])
    mult = np.zeros(d.shape, np.float64)
    for window, dil in DILATED_PATTERNS:
        mult += (d >= 0) & (d <= window) & (d % dil == 0)
    with np.errstate(divide="ignore"):
        tab = np.where(mult > 0, np.log(np.maximum(mult, 1.0)), NEG_BIG)
    return jnp.asarray(tab, F32)


def _hgrn_kernel(q_ref, f_ref, i_ref, g_ref, lb_ref, nw_ref, o_ref, st_ref):
    C, S, E, V = HG_CHUNK, HG_SUB, HG_E, HG_V
    NS = C // S

    @pl.when(pl.program_id(1) == 0)
    def _():
        st_ref[...] = jnp.zeros_like(st_ref)

    r = lax.broadcasted_iota(jnp.int32, (C, C), 0)
    c = lax.broadcasted_iota(jnp.int32, (C, C), 1)
    tril = (c <= r).astype(F32)
    same_or_before = c <= r
    earlier_sub = (c // S) < (r // S)
    sub_base = (r // S) * S
    row_sub = lax.broadcasted_iota(jnp.int32, (C, E), 0) // S

    def chunk(ci, _):
        r0 = pl.multiple_of(ci * C, C)
        for hd in range(HG_HEADS):
            lanes = slice(hd * E, (hd + 1) * E)
            z = f_ref[0, pl.ds(r0, C), lanes]
            qr = q_ref[0, pl.ds(r0, C), lanes]
            v = i_ref[0, pl.ds(r0, C), lanes]
            gate = g_ref[0, pl.ds(r0, C), lanes]
            lbf = jnp.clip(lb_ref[:, lanes], 0.0, 1.0 - 1e-6)
            ez = jnp.exp(-jnp.abs(z))
            inv = 1.0 / (1.0 + ez)
            sig_pos = jnp.where(z >= 0, inv, ez * inv)
            sig_neg = jnp.where(z >= 0, ez * inv, inv)
            lf = jnp.log(jnp.maximum(lbf, LB_FLOOR) + (1.0 - lbf) * sig_pos)
            k = (1.0 - lbf) * sig_neg
            q = qr * _sigmoid(qr)
            b = jnp.dot(tril, lf, precision=lax.Precision.HIGHEST,
                        preferred_element_type=F32)
            b_last = b[C - 1:C, :]
            st = st_ref[hd]
            qe = (q * jnp.exp(b)).astype(BF16)
            inter = lax.dot_general(qe, st.astype(BF16), (((1,), (1,)), ((), ())),
                                    preferred_element_type=F32)
            kdec = (k * jnp.exp(b_last - b)).astype(BF16)
            st_ref[hd] = st * jnp.exp(b_last) + jnp.dot(
                v.T.astype(BF16), kdec, preferred_element_type=F32)

            b4 = b.reshape(NS, S, E)
            k4 = k.reshape(NS, S, E)
            q4 = q.reshape(NS, S, E)
            b_end = b4[:, S - 1:S, :]
            k_hat = (k4 * jnp.exp(b_end - b4)).reshape(C, E)
            q_parts, k_parts = [], []
            for j in range(NS - 1):
                anchor = b[(j + 1) * S - 1:(j + 1) * S, :]
                q_parts.append((q * jnp.exp(jnp.minimum(b - anchor, 0.0))).astype(BF16))
                k_parts.append(jnp.where(row_sub == j, k_hat, 0.0).astype(BF16))
            a_off = lax.dot_general(jnp.concatenate(q_parts, axis=1),
                                    jnp.concatenate(k_parts, axis=1),
                                    (((1,), (1,)), ((), ())),
                                    preferred_element_type=F32)

            a_diag = jnp.zeros((C, C), F32)
            for s in range(S):
                w = jnp.exp(jnp.minimum(b4 - b4[:, s:s + 1, :], 0.0)) * (q4 * k4[:, s:s + 1, :])
                col_sum = jnp.sum(w, axis=-1, keepdims=True).reshape(C, 1)
                a_diag = a_diag + jnp.where(c == sub_base + s, col_sum, 0.0)

            a = jnp.where(earlier_sub, a_off, jnp.where(same_or_before, a_diag, 0.0))
            o = inter + jnp.dot(a.astype(BF16), v.astype(BF16), preferred_element_type=F32)
            o = o * lax.rsqrt(jnp.mean(o * o, axis=-1, keepdims=True) + EPS)
            o = o * nw_ref[:, lanes] * _sigmoid(gate)
            o_ref[0, pl.ds(r0, C), lanes] = o.astype(BF16)
        return 0

    lax.fori_loop(0, HG_TB // C, chunk, 0)


def _hgrn(hq, hf, hi, hg, lb, nw):
    B, T, W = hq.shape
    row = lambda b, t: (b, t, 0)
    const = lambda b, t: (0, 0)
    blk = pl.BlockSpec((1, HG_TB, W), row)
    return pl.pallas_call(
        _hgrn_kernel,
        grid=(B, T // HG_TB),
        in_specs=[blk, blk, blk, blk, pl.BlockSpec((1, W), const), pl.BlockSpec((1, W), const)],
        out_specs=pl.BlockSpec((1, HG_TB, W), row),
        out_shape=jax.ShapeDtypeStruct((B, T, W), BF16),
        scratch_shapes=[pltpu.VMEM((HG_HEADS, HG_V, HG_E), F32)],
        compiler_params=_params(("parallel", "arbitrary")),
        name="hgrn2",
    )(hq, hf, hi, hg, lb, nw)


def _outproj_kernel(x_ref, oa_ref, ob_ref, oc_ref, w_ref, o_ref):
    W = ATT_W
    y = jnp.dot(oa_ref[...], w_ref[0:W, :], preferred_element_type=F32)
    y += jnp.dot(ob_ref[...], w_ref[W:2 * W, :], preferred_element_type=F32)
    y += jnp.dot(oc_ref[...], w_ref[2 * W:, :], preferred_element_type=F32)
    o_ref[...] = x_ref[...] + y


def _outproj(x2d, oa, ob, oc, w):
    n = x2d.shape[0]
    row = lambda i: (i, 0)
    return pl.pallas_call(
        _outproj_kernel,
        grid=(n // TM_PROJ,),
        in_specs=[
            pl.BlockSpec((TM_PROJ, D_MODEL), row),
            pl.BlockSpec((TM_PROJ, ATT_W), row),
            pl.BlockSpec((TM_PROJ, ATT_W), row),
            pl.BlockSpec((TM_PROJ, HG_W), row),
            pl.BlockSpec(w.shape, lambda i: (0, 0), pipeline_mode=pl.Buffered(1)),
        ],
        out_specs=pl.BlockSpec((TM_PROJ, D_MODEL), row),
        out_shape=jax.ShapeDtypeStruct((n, D_MODEL), F32),
        compiler_params=_params(("parallel",)),
        name="outproj",
    )(x2d, oa, ob, oc, w)


def _rot_cols(w):
    half = ROPE_DIM // 2
    w4 = w.reshape(w.shape[0], N_HEADS, HEAD_DIM)
    rot = jnp.concatenate([-w4[..., half:ROPE_DIM], w4[..., :half],
                           jnp.zeros_like(w4[..., ROPE_DIM:])], axis=-1)
    return rot.reshape(w.shape)


def _rope_tables(positions):
    half = ROPE_DIM // 2
    freqs = ROPE_THETA ** (-jnp.arange(0, ROPE_DIM, 2, dtype=F32) / ROPE_DIM)
    ang = positions.astype(F32)[:, :, None] * freqs
    B, T = positions.shape
    ones = jnp.ones((B, T, HEAD_DIM - ROPE_DIM), F32)
    cos_h = jnp.concatenate([jnp.cos(ang), jnp.cos(ang), ones], axis=-1)
    sin_h = jnp.concatenate([jnp.sin(ang), jnp.sin(ang), 0.0 * ones], axis=-1)
    cos = jnp.tile(cos_h, (1, 1, N_HEADS))
    sin = jnp.tile(sin_h, (1, 1, N_HEADS))
    return cos, sin, cos.transpose(0, 2, 1), sin.transpose(0, 2, 1)


def _split_w_in(w):
    W = ATT_W
    sizes = (W, W, W, N_HEADS, W, W, W, HG_W, HG_W, HG_W, HG_W)
    pts = tuple(int(s) for s in np.cumsum(sizes)[:-1])
    return jnp.split(w, pts, axis=1)


def kernel(x, positions, ffn1_norm, ffn1_w_gate, ffn1_w_up, ffn1_w_down, mix_norm, w_in,
           fox_forget_bias, hgrn_lower_bounds, hgrn_out_norm, w_out, ffn2_norm,
           ffn2_w_gate, ffn2_w_up, ffn2_w_down, final_norm):
    B, T, D = x.shape
    depth = w_in.shape[0]
    assert D == D_MODEL and T % max(TM_PROJ, HG_TB, BQ) == 0 and (B * T) % TM_FFN == 0
    scale = HEAD_DIM ** -0.5

    sm = jax.nn.softmax(hgrn_lower_bounds.astype(F32), axis=0)
    lbs = jnp.cumsum(sm, axis=0) - sm[0:1]
    cos, sin, cost, sint = _rope_tables(positions)
    dil_tab = _dilated_bias_table()
    fnorm = final_norm.reshape(1, D)

    xf = x.reshape(B * T, D)
    for i in range(depth):
        xf = _ffn(xf, ffn1_norm[i].reshape(1, D), ffn1_w_gate[i].astype(BF16),
                  ffn1_w_up[i].astype(BF16), ffn1_w_down[i].astype(BF16), fnorm, False)

        fq, fk, fv, ffw, dq, dk, dv, hq, hf, hi, hg = _split_w_in(w_in[i])
        ffw_pad = jnp.pad(ffw, ((0, 0), (0, FF_LANES - N_HEADS)))
        wa = jnp.concatenate([fk, dk, _rot_cols(dk), ffw_pad, hq, hf, hi, hg], axis=1).astype(BF16)
        wbt = jnp.concatenate([fq * scale, fv, dq * scale, _rot_cols(dq) * scale, dv],
                              axis=1).T.astype(BF16)
        fb = jnp.pad(fox_forget_bias[i].astype(F32), (0, FF_LANES - N_HEADS)).reshape(1, FF_LANES)
        (fk_a, lf_a, dk_a, hq_a, hf_a, hi_a, hg_a, fqt_a, fvt_a, dqt_a, dvt_a) = _inproj(
            xf.reshape(B, T, D), mix_norm[i].reshape(1, D), wa, wbt, cos, sin, cost, sint, fb)

        oa = _attention(True, fqt_a, fk_a, fvt_a, lf_a)
        ob = _attention(False, dqt_a, dk_a, dvt_a, dil_tab)
        oc = _hgrn(hq_a, hf_a, hi_a, hg_a, lbs[i].reshape(1, HG_W),
                   hgrn_out_norm[i].astype(F32).reshape(1, HG_W))
        xf = _outproj(xf, oa.reshape(B * T, ATT_W), ob.reshape(B * T, ATT_W),
                      oc.reshape(B * T, HG_W), w_out[i].astype(BF16))

        xf = _ffn(xf, ffn2_norm[i].reshape(1, D), ffn2_w_gate[i].astype(BF16),
                  ffn2_w_up[i].astype(BF16), ffn2_w_down[i].astype(BF16), fnorm,
                  i == depth - 1)
    return xf.reshape(B, T, D)
```

```python
import functools
import math

import numpy as np
import jax
import jax.numpy as jnp
from jax import lax
from jax.experimental import pallas as pl
from jax.experimental.pallas import tpu as pltpu

F32 = jnp.float32
BF16 = jnp.bfloat16

D_MODEL = 1024
HEAD_DIM = 64
N_HEADS = 4
ATT_W = N_HEADS * HEAD_DIM
HG_HEADS = 4
HG_E = 128
HG_V = 128
HG_W = HG_HEADS * HG_V
ROPE_DIM = HEAD_DIM // 4
ROPE_THETA = 500000.0
D_FF = ((8 * D_MODEL // 3 + 127) // 128) * 128
EPS = 1e-6
NEG_BIG = -1e30
LB_FLOOR = 1e-30
DILATED_PATTERNS = ((128, 1), (512, 4), (2048, 16))
FF_LANES = 128
VT_ROWS = HEAD_DIM + 16
LOG2E = 1.0 / math.log(2.0)

V7X_VMEM_BYTES = 64 * 1024 * 1024
VMEM_LIMIT = V7X_VMEM_BYTES - 8 * 1024 * 1024
SUBLANES = 8

TM_FFN = 512
FF_CHUNK = 512
TM_PROJ = 512
BQ = 512
BK = 256
HG_CHUNK = 64
HG_SUB = SUBLANES
HG_TB = 1024
CS_BLK = 256


def _rms(x, g):
    ms = jnp.mean(x * x, axis=-1, keepdims=True)
    return x * lax.rsqrt(ms + EPS) * g


def _sigmoid(x):
    return 1.0 / (1.0 + jnp.exp(-x))


def _params(sem):
    return pltpu.CompilerParams(dimension_semantics=sem, vmem_limit_bytes=VMEM_LIMIT)


def _ffn_kernel(final, x_ref, g_ref, wg_ref, wu_ref, wd_ref, fg_ref, o_ref, act_ref):
    x = x_ref[...]
    h = _rms(x, g_ref[...]).astype(BF16)
    for c in range(0, D_FF, FF_CHUNK):
        w = min(FF_CHUNK, D_FF - c)
        g = jnp.dot(h, wg_ref[:, c:c + w], preferred_element_type=F32)
        u = jnp.dot(h, wu_ref[:, c:c + w], preferred_element_type=F32)
        act_ref[:, c:c + w] = (g * _sigmoid(g) * u).astype(BF16)
    y = x + 0.5 * jnp.dot(act_ref[...], wd_ref[...], preferred_element_type=F32)
    if final:
        y = _rms(y, fg_ref[...])
    o_ref[...] = y


def _ffn(x2d, g, wg, wu, wd, fg, final):
    n = x2d.shape[0]
    const = lambda i: (0, 0)
    return pl.pallas_call(
        functools.partial(_ffn_kernel, final),
        grid=(n // TM_FFN,),
        in_specs=[
            pl.BlockSpec((TM_FFN, D_MODEL), lambda i: (i, 0)),
            pl.BlockSpec((1, D_MODEL), const),
            pl.BlockSpec((D_MODEL, D_FF), const, pipeline_mode=pl.Buffered(1)),
            pl.BlockSpec((D_MODEL, D_FF), const, pipeline_mode=pl.Buffered(1)),
            pl.BlockSpec((D_FF, D_MODEL), const, pipeline_mode=pl.Buffered(1)),
            pl.BlockSpec((1, D_MODEL), const),
        ],
        out_specs=pl.BlockSpec((TM_FFN, D_MODEL), lambda i: (i, 0)),
        out_shape=jax.ShapeDtypeStruct((n, D_MODEL), F32),
        scratch_shapes=[pltpu.VMEM((TM_FFN, D_FF), BF16)],
        compiler_params=_params(("parallel",)),
        name="ffn_final" if final else "ffn",
    )(x2d, g, wg, wu, wd, fg)


def _inproj_kernel(x_ref, g_ref, wa_ref, wbt_ref, cos_ref, sin_ref, cost_ref, sint_ref, fb_ref,
                   lb_ref,
                   fk_ref, lf_ref, dk_ref, hq_ref, hl_ref, hk_ref, hv_ref, hs_ref,
                   fqt_ref, fvt_ref, dqt_ref, dvt_ref):
    h = _rms(x_ref[0], g_ref[...]).astype(BF16)
    W = ATT_W
    pa = jnp.dot(h, wa_ref[:, 0:3 * W + FF_LANES], preferred_element_type=F32)
    fk_ref[0] = pa[:, 0:W].astype(BF16)
    dk_ref[0] = (pa[:, W:2 * W] * cos_ref[0] + pa[:, 2 * W:3 * W] * sin_ref[0]).astype(BF16)
    z = pa[:, 3 * W:3 * W + FF_LANES] + fb_ref[...]
    lf_ref[0] = (jnp.minimum(z, 0.0) - jnp.log1p(jnp.exp(-jnp.abs(z)))) * LOG2E

    base = 3 * W + FF_LANES
    hdot = lambda i: jnp.dot(h, wa_ref[:, base + i * HG_W:base + (i + 1) * HG_W],
                             preferred_element_type=F32)
    q = hdot(0)
    hq_ref[0] = q * _sigmoid(q)
    z = hdot(1)
    lbf = jnp.clip(lb_ref[...], 0.0, 1.0 - 1e-6)
    ez = jnp.exp(-jnp.abs(z))
    inv = 1.0 / (1.0 + ez)
    sig_pos = jnp.where(z >= 0, inv, ez * inv)
    sig_neg = jnp.where(z >= 0, ez * inv, inv)
    hl_ref[0] = jnp.log(jnp.maximum(lbf, LB_FLOOR) + (1.0 - lbf) * sig_pos) * LOG2E
    hk_ref[0] = (1.0 - lbf) * sig_neg
    hv_ref[0] = hdot(2)
    hs_ref[0] = _sigmoid(hdot(3))

    pb = lax.dot_general(wbt_ref[...], h, (((1,), (1,)), ((), ())),
                         preferred_element_type=F32)
    fqt_ref[0] = pb[0:W].astype(BF16)
    dqt_ref[0] = (pb[2 * W:3 * W] * cost_ref[0] + pb[3 * W:4 * W] * sint_ref[0]).astype(BF16)
    ones = jnp.ones((VT_ROWS - HEAD_DIM, BK), BF16)
    for src, ref in ((pb[W:2 * W].astype(BF16), fvt_ref), (pb[4 * W:5 * W].astype(BF16), dvt_ref)):
        for j in range(TM_PROJ // BK):
            ref[0, j] = jnp.concatenate(
                [part for h in range(N_HEADS)
                 for part in (src[h * HEAD_DIM:(h + 1) * HEAD_DIM, j * BK:(j + 1) * BK], ones)],
                axis=0)


def _inproj(x, g, wa, wbt, cos, sin, cost, sint, fb, lb):
    B, T, _ = x.shape
    W = ATT_W
    nt = T // TM_PROJ
    const = lambda b, t: (0, 0)
    row = lambda b, t: (b, t, 0)
    col = lambda b, t: (b, 0, t)
    blk4 = lambda b, t: (b, t, 0, 0)
    hg_shape = jax.ShapeDtypeStruct((B, T, HG_W), F32)
    hg_spec = pl.BlockSpec((1, TM_PROJ, HG_W), row)
    out_shape = [
        jax.ShapeDtypeStruct((B, T, W), BF16),
        jax.ShapeDtypeStruct((B, T, FF_LANES), F32),
        jax.ShapeDtypeStruct((B, T, W), BF16),
        hg_shape, hg_shape, hg_shape, hg_shape, hg_shape,
        jax.ShapeDtypeStruct((B, W, T), BF16),
        jax.ShapeDtypeStruct((B, T // BK, N_HEADS * VT_ROWS, BK), BF16),
        jax.ShapeDtypeStruct((B, W, T), BF16),
        jax.ShapeDtypeStruct((B, T // BK, N_HEADS * VT_ROWS, BK), BF16),
    ]
    out_specs = [
        pl.BlockSpec((1, TM_PROJ, W), row),
        pl.BlockSpec((1, TM_PROJ, FF_LANES), row),
        pl.BlockSpec((1, TM_PROJ, W), row),
        hg_spec, hg_spec, hg_spec, hg_spec, hg_spec,
        pl.BlockSpec((1, W, TM_PROJ), col),
        pl.BlockSpec((1, TM_PROJ // BK, N_HEADS * VT_ROWS, BK), blk4),
        pl.BlockSpec((1, W, TM_PROJ), col),
        pl.BlockSpec((1, TM_PROJ // BK, N_HEADS * VT_ROWS, BK), blk4),
    ]
    return pl.pallas_call(
        _inproj_kernel,
        grid=(B, nt),
        in_specs=[
            pl.BlockSpec((1, TM_PROJ, D_MODEL), row),
            pl.BlockSpec((1, D_MODEL), const),
            pl.BlockSpec(wa.shape, const, pipeline_mode=pl.Buffered(1)),
            pl.BlockSpec(wbt.shape, const, pipeline_mode=pl.Buffered(1)),
            pl.BlockSpec((1, TM_PROJ, W), row),
            pl.BlockSpec((1, TM_PROJ, W), row),
            pl.BlockSpec((1, W, TM_PROJ), col),
            pl.BlockSpec((1, W, TM_PROJ), col),
            pl.BlockSpec((1, FF_LANES), const),
            pl.BlockSpec((1, HG_W), const),
        ],
        out_specs=out_specs,
        out_shape=out_shape,
        compiler_params=_params(("parallel", "parallel")),
        name="inproj",
    )(x, g, wa, wbt, cos, sin, cost, sint, fb, lb)


def _attn_kernel(fox, qt_ref, k_ref, vt_ref, aux_ref, o_ref, qm_ref, acc_ref, st0_ref, st1_ref,
                 *scratch):
    qi = pl.program_id(1)
    T = k_ref.shape[1]
    R = BQ // BK
    assert R == 2
    qt = qt_ref[0]

    if fox:
        ka_ref, = scratch
        n_split = 3

        @pl.when(qi == 0)
        def _():
            r = lax.broadcasted_iota(jnp.int32, (CS_BLK, CS_BLK), 0)
            c = lax.broadcasted_iota(jnp.int32, (CS_BLK, CS_BLK), 1)
            tril = (c <= r).astype(F32)
            lane = lax.broadcasted_iota(jnp.int32, (CS_BLK, HEAD_DIM), 1)
            carry = jnp.zeros((1, FF_LANES), F32)
            for i in range(T // CS_BLK):
                rows = slice(i * CS_BLK, (i + 1) * CS_BLK)
                blk = jnp.dot(tril, aux_ref[0, rows, :], precision=lax.Precision.HIGHEST,
                              preferred_element_type=F32) + carry
                carry = blk[CS_BLK - 1:CS_BLK, :]
                for h in range(N_HEADS):
                    rest = jnp.broadcast_to(blk[:, h:h + 1], (CS_BLK, HEAD_DIM))
                    extra = jnp.zeros((CS_BLK, HEAD_DIM), F32)
                    for piece in range(n_split):
                        part = rest.astype(BF16).astype(F32)
                        extra = jnp.where(lane == piece, part, extra)
                        rest = rest - part
                    ka_ref[rows, 2 * h * HEAD_DIM:(2 * h + 1) * HEAD_DIM] = (
                        k_ref[0, rows, h * HEAD_DIM:(h + 1) * HEAD_DIM])
                    ka_ref[rows, (2 * h + 1) * HEAD_DIM:(2 * h + 2) * HEAD_DIM] = extra.astype(BF16)

        pad_row = lax.broadcasted_iota(jnp.int32, (HEAD_DIM, BQ), 0)
        minus_ones = jnp.where(pad_row < n_split, -1.0, 0.0).astype(BF16)
        for h in range(N_HEADS):
            qm_ref[h] = jnp.concatenate([qt[h * HEAD_DIM:(h + 1) * HEAD_DIM], minus_ones], axis=0)
    else:
        head_of_row = lax.broadcasted_iota(jnp.int32, qt.shape, 0) // HEAD_DIM
        for h in range(N_HEADS):
            qm_ref[h] = jnp.where(head_of_row == h, qt, jnp.zeros_like(qt))
    acc_ref[...] = jnp.zeros_like(acc_ref)
    krow = lax.broadcasted_iota(jnp.int32, (BK, BQ), 0)
    qcol = lax.broadcasted_iota(jnp.int32, (BK, BQ), 1)

    st_slots = (st0_ref, st1_ref)

    def fetch(j):
        start = pl.multiple_of(j * BK, BK)
        if fox:
            kbs = [ka_ref[pl.ds(start, BK), 2 * h * HEAD_DIM:2 * (h + 1) * HEAD_DIM]
                   for h in range(N_HEADS)]
            aux = None
        else:
            kbs = [k_ref[0, pl.ds(start, BK), :]] * N_HEADS
            aux = jnp.concatenate([aux_ref[R * qi + half - j + 1] for half in range(R)], axis=1)
        vbs = [vt_ref[0, j, h * VT_ROWS:(h + 1) * VT_ROWS, :] for h in range(N_HEADS)]
        return kbs, vbs, aux

    def stage(blk, slot, ms, nxt=None, diag=None):
        _, vbs, aux = blk
        new_ms = []
        for h in range(N_HEADS):
            if nxt is not None:
                st_slots[1 - slot][h] = jnp.dot(nxt[0][h], qm_ref[h], preferred_element_type=F32)
            st = st_slots[slot][h]
            if not fox:
                st = st + aux
            elif diag is not None:
                st = jnp.where(krow + diag * BK <= qcol, st, NEG_BIG)
            m_new = jnp.maximum(ms[h], jnp.max(st, axis=0, keepdims=True))
            p = jnp.exp2(st - m_new)
            alpha = jnp.exp2(ms[h] - m_new)
            new_ms.append(m_new)
            rows = slice(h * VT_ROWS, (h + 1) * VT_ROWS)
            acc_ref[rows, :] = alpha * acc_ref[rows, :] + jnp.dot(
                vbs[h], p.astype(BF16), preferred_element_type=F32)
        return tuple(new_ms)

    def pair(i, ms):
        b0, b1, b2 = fetch(2 * i), fetch(2 * i + 1), fetch(2 * i + 2)
        ms = stage(b0, 0, ms, nxt=b1)
        return stage(b1, 1, ms, nxt=b2)

    ms = tuple(jnp.full((1, BQ), NEG_BIG, F32) for _ in range(N_HEADS))
    first = fetch(0)
    for h in range(N_HEADS):
        st_slots[0][h] = jnp.dot(first[0][h], qm_ref[h], preferred_element_type=F32)
    ms = lax.fori_loop(0, qi, pair, ms)
    b0, b1 = fetch(R * qi), fetch(R * qi + 1)
    ms = stage(b0, 0, ms, nxt=b1, diag=0)
    stage(b1, 1, ms, diag=1)
    outs = []
    for h in range(N_HEADS):
        num = acc_ref[h * VT_ROWS:h * VT_ROWS + HEAD_DIM, :]
        den = acc_ref[h * VT_ROWS + HEAD_DIM:h * VT_ROWS + HEAD_DIM + 1, :]
        outs.append(num / den)
    o_ref[0] = jnp.concatenate(outs, axis=0).T.astype(BF16)


def _attention(fox, qt, k, vt, aux):
    B, T, W = k.shape
    nq = T // BQ
    scratch = [pltpu.VMEM((N_HEADS, 2 * HEAD_DIM if fox else W, BQ), BF16),
               pltpu.VMEM((N_HEADS * VT_ROWS, BQ), F32),
               pltpu.VMEM((N_HEADS, BK, BQ), F32), pltpu.VMEM((N_HEADS, BK, BQ), F32)]
    if fox:
        aux_spec = pl.BlockSpec((1, T, FF_LANES), lambda b, q: (b, 0, 0))
        scratch.append(pltpu.VMEM((T, N_HEADS * 2 * HEAD_DIM), BF16))
    else:
        aux_spec = pl.BlockSpec(aux.shape, lambda b, q: (0, 0, 0))
    return pl.pallas_call(
        functools.partial(_attn_kernel, fox),
        grid=(B, nq),
        in_specs=[
            pl.BlockSpec((1, W, BQ), lambda b, q: (b, 0, q)),
            pl.BlockSpec((1, T, W), lambda b, q: (b, 0, 0)),
            pl.BlockSpec((1, T // BK, N_HEADS * VT_ROWS, BK), lambda b, q: (b, 0, 0, 0)),
            aux_spec,
        ],
        out_specs=pl.BlockSpec((1, BQ, W), lambda b, q: (b, q, 0)),
        out_shape=jax.ShapeDtypeStruct((B, T, W), BF16),
        scratch_shapes=scratch,
        compiler_params=_params(("parallel", "arbitrary")),
        name="fox_attention" if fox else "dilated_attention",
    )(qt, k, vt, aux)


def _dilated_bias_table():
    nd = max(w for w, _ in DILATED_PATTERNS) // BK
    d = ((np.arange(nd + 1)[:, None, None] - 1) * BK + np.arange(BK)[None, None, :]
         - np.arange(BK)[None, :, None])
    mult = np.zeros(d.shape, np.float64)
    for window, dil in DILATED_PATTERNS:
        mult += (d >= 0) & (d <= window) & (d % dil == 0)
    tab = np.where(mult > 0, np.log2(np.maximum(mult, 1.0)), NEG_BIG)
    return jnp.asarray(tab, F32)


def _hgrn_kernel(q_ref, lf_ref, k_ref, v_ref, s_ref, nw_ref, o_ref,
                 st_ref, b_scr, k_scr, a_scr, g_scr, bn_scr, ta_scr, ti_scr, tv_scr, tg_scr):
    C, S, E = HG_CHUNK, HG_SUB, HG_E
    NS = C // S
    n_chunks = HG_TB // C
    heads = range(HG_HEADS)
    lanes = [slice(hd * E, (hd + 1) * E) for hd in heads]

    @pl.when(pl.program_id(1) == 0)
    def _():
        st_ref[...] = jnp.zeros_like(st_ref)

    r = lax.broadcasted_iota(jnp.int32, (C, C), 0)
    c = lax.broadcasted_iota(jnp.int32, (C, C), 1)
    tril = (c <= r).astype(F32)
    diag_mask = ((c // S) == (r // S)) & (c <= r)
    rr = lax.broadcasted_iota(jnp.int32, (S * E, C), 0)
    rc = lax.broadcasted_iota(jnp.int32, (S * E, C), 1)
    red = jnp.where(rr // E == rc % S, 1.0, 0.0).astype(BF16)
    first_row = lax.broadcasted_iota(jnp.int32, (NS, E), 0) == 0
    zeros_sub = jnp.zeros((S, E), F32)

    def rows_bcast(ref, hd, idx):
        return jnp.concatenate(
            [jnp.broadcast_to(ref[hd, idx(i):idx(i) + 1, :], (S, E)) for i in range(NS)], axis=0)

    def cumsum(ci):
        r0 = pl.multiple_of(ci * C, C)
        return [jnp.dot(tril, lf_ref[0, pl.ds(r0, C), lanes[hd]], precision=lax.Precision.HIGHEST,
                        preferred_element_type=F32) for hd in heads]

    def out_dots():
        return [ti_scr[hd] + jnp.dot(ta_scr[hd], tv_scr[hd], preferred_element_type=F32)
                for hd in heads]

    def out_finish(ci, os_, gates):
        r0 = pl.multiple_of(ci * C, C)
        for hd in heads:
            o = os_[hd]
            o = o * lax.rsqrt(jnp.mean(o * o, axis=-1, keepdims=True) + EPS)
            o_ref[0, pl.ds(r0, C), lanes[hd]] = (o * gates[hd]).astype(BF16)

    def scores(ci, bs_):
        r0 = pl.multiple_of(ci * C, C)
        for hd in heads:
            b = bs_[hd]
            q = q_ref[0, pl.ds(r0, C), lanes[hd]]
            k = k_ref[0, pl.ds(r0, C), lanes[hd]]
            v = v_ref[0, pl.ds(r0, C), lanes[hd]]
            b_last = b[C - 1:C, :]
            st = st_ref[hd]
            qe = (q * jnp.exp2(b)).astype(BF16)
            ti_scr[hd] = lax.dot_general(qe, st.astype(BF16), (((1,), (1,)), ((), ())),
                                         preferred_element_type=F32)
            kdec = (k * jnp.exp2(b_last - b)).astype(BF16)
            st_ref[hd] = st * jnp.exp2(b_last) + jnp.dot(
                v.T.astype(BF16), kdec, preferred_element_type=F32)
            tv_scr[hd] = v.astype(BF16)
            tg_scr[hd] = s_ref[0, pl.ds(r0, C), lanes[hd]] * nw_ref[:, lanes[hd]]
            b_scr[hd] = b
            k_scr[hd] = k
            b_end = b_scr[hd, pl.ds(S - 1, NS, stride=S), :]
            b_prev = jnp.where(first_row, 0.0, pltpu.roll(b_end, 1, axis=0))
            a_scr[hd, 0:NS] = b_end
            a_scr[hd, NS:2 * NS] = b_prev
            q_t = q * jnp.exp2(b - rows_bcast(a_scr, hd, lambda i: NS + i))
            k_h = k * jnp.exp2(rows_bcast(a_scr, hd, lambda i: i) - b)
            for j in range(NS - 1):
                g_scr[hd, j] = jnp.exp2(jnp.minimum(b_prev - b_end[j:j + 1, :], 0.0))
            lhs_rows, rhs_rows = [], []
            for i in range(NS):
                qi_t = q_t[i * S:(i + 1) * S, :]
                lhs_rows.append(jnp.concatenate(
                    [qi_t * g_scr[hd, j, i:i + 1, :] if i > j else zeros_sub
                     for j in range(NS - 1)], axis=1))
                rhs_rows.append(jnp.concatenate(
                    [k_h[i * S:(i + 1) * S, :] if i == j else zeros_sub
                     for j in range(NS - 1)], axis=1))
            a_off = lax.dot_general(jnp.concatenate(lhs_rows, axis=0).astype(BF16),
                                    jnp.concatenate(rhs_rows, axis=0).astype(BF16),
                                    (((1,), (1,)), ((), ())),
                                    preferred_element_type=F32)
            w_parts = []
            for s in range(S):
                bs = rows_bcast(b_scr, hd, lambda i: i * S + s)
                ks = rows_bcast(k_scr, hd, lambda i: i * S + s)
                w_parts.append((jnp.exp2(jnp.minimum(b - bs, 0.0)) * (q * ks)).astype(BF16))
            a_diag = jnp.dot(jnp.concatenate(w_parts, axis=1), red,
                             preferred_element_type=F32)
            ta_scr[hd] = (a_off + jnp.where(diag_mask, a_diag, 0.0)).astype(BF16)

    def step(ci, _):
        bs_ = [bn_scr[hd] for hd in heads]
        gates = [tg_scr[hd] for hd in heads]
        nxt = cumsum(jnp.minimum(ci + 1, n_chunks - 1))
        os_ = out_dots()
        scores(ci, bs_)
        out_finish(jnp.maximum(ci - 1, 0), os_, gates)
        for hd in heads:
            bn_scr[hd] = nxt[hd]
        return 0

    for ref in (ta_scr, ti_scr, tv_scr, tg_scr):
        ref[...] = jnp.zeros_like(ref)
    first = cumsum(0)
    for hd in heads:
        bn_scr[hd] = first[hd]
    lax.fori_loop(0, n_chunks, step, 0)
    out_finish(n_chunks - 1, out_dots(), [tg_scr[hd] for hd in heads])


def _hgrn(hq, hl, hk, hv, hs, nw):
    B, T, W = hq.shape
    row = lambda b, t: (b, t, 0)
    blk = pl.BlockSpec((1, HG_TB, W), row)
    C, NS = HG_CHUNK, HG_CHUNK // HG_SUB
    per_head = lambda *shape, dtype=F32: pltpu.VMEM((HG_HEADS,) + shape, dtype)
    return pl.pallas_call(
        _hgrn_kernel,
        grid=(B, T // HG_TB),
        in_specs=[blk, blk, blk, blk, blk, pl.BlockSpec((1, W), lambda b, t: (0, 0))],
        out_specs=pl.BlockSpec((1, HG_TB, W), row),
        out_shape=jax.ShapeDtypeStruct((B, T, W), BF16),
        scratch_shapes=[
            per_head(HG_V, HG_E),
            per_head(C, HG_E),
            per_head(C, HG_E),
            per_head(2 * NS, HG_E),
            per_head(NS - 1, NS, HG_E),
            per_head(C, HG_E),
            per_head(C, C, dtype=BF16),
            per_head(C, HG_V),
            per_head(C, HG_V, dtype=BF16),
            per_head(C, HG_V),
        ],
        compiler_params=_params(("parallel", "arbitrary")),
        name="hgrn2",
    )(hq, hl, hk, hv, hs, nw)


def _outproj_kernel(x_ref, oa_ref, ob_ref, oc_ref, w_ref, o_ref):
    W = ATT_W
    y = jnp.dot(oa_ref[...], w_ref[0:W, :], preferred_element_type=F32)
    y += jnp.dot(ob_ref[...], w_ref[W:2 * W, :], preferred_element_type=F32)
    y += jnp.dot(oc_ref[...], w_ref[2 * W:, :], preferred_element_type=F32)
    o_ref[...] = x_ref[...] + y


def _outproj(x2d, oa, ob, oc, w):
    n = x2d.shape[0]
    row = lambda i: (i, 0)
    return pl.pallas_call(
        _outproj_kernel,
        grid=(n // TM_PROJ,),
        in_specs=[
            pl.BlockSpec((TM_PROJ, D_MODEL), row),
            pl.BlockSpec((TM_PROJ, ATT_W), row),
            pl.BlockSpec((TM_PROJ, ATT_W), row),
            pl.BlockSpec((TM_PROJ, HG_W), row),
            pl.BlockSpec(w.shape, lambda i: (0, 0), pipeline_mode=pl.Buffered(1)),
        ],
        out_specs=pl.BlockSpec((TM_PROJ, D_MODEL), row),
        out_shape=jax.ShapeDtypeStruct((n, D_MODEL), F32),
        compiler_params=_params(("parallel",)),
        name="outproj",
    )(x2d, oa, ob, oc, w)


def _rot_cols(w):
    half = ROPE_DIM // 2
    w4 = w.reshape(w.shape[0], N_HEADS, HEAD_DIM)
    rot = jnp.concatenate([-w4[..., half:ROPE_DIM], w4[..., :half],
                           jnp.zeros_like(w4[..., ROPE_DIM:])], axis=-1)
    return rot.reshape(w.shape)


def _rope_tables(positions):
    freqs = ROPE_THETA ** (-jnp.arange(0, ROPE_DIM, 2, dtype=F32) / ROPE_DIM)
    ang = positions.astype(F32)[:, :, None] * freqs
    B, T = positions.shape
    ones = jnp.ones((B, T, HEAD_DIM - ROPE_DIM), F32)
    cos_h = jnp.concatenate([jnp.cos(ang), jnp.cos(ang), ones], axis=-1)
    sin_h = jnp.concatenate([jnp.sin(ang), jnp.sin(ang), 0.0 * ones], axis=-1)
    cos = jnp.tile(cos_h, (1, 1, N_HEADS))
    sin = jnp.tile(sin_h, (1, 1, N_HEADS))
    return cos, sin, cos.transpose(0, 2, 1), sin.transpose(0, 2, 1)


def _split_w_in(w):
    W = ATT_W
    sizes = (W, W, W, N_HEADS, W, W, W, HG_W, HG_W, HG_W, HG_W)
    pts = tuple(int(s) for s in np.cumsum(sizes)[:-1])
    return jnp.split(w, pts, axis=1)


def kernel(x, positions, ffn1_norm, ffn1_w_gate, ffn1_w_up, ffn1_w_down, mix_norm, w_in,
           fox_forget_bias, hgrn_lower_bounds, hgrn_out_norm, w_out, ffn2_norm,
           ffn2_w_gate, ffn2_w_up, ffn2_w_down, final_norm):
    B, T, D = x.shape
    depth = w_in.shape[0]
    assert D == D_MODEL and T % max(TM_PROJ, HG_TB, BQ) == 0 and (B * T) % TM_FFN == 0
    assert BQ % BK == 0 and TM_PROJ % BK == 0
    scale = HEAD_DIM ** -0.5 * LOG2E

    sm = jax.nn.softmax(hgrn_lower_bounds.astype(F32), axis=0)
    lbs = jnp.cumsum(sm, axis=0) - sm[0:1]
    cos, sin, cost, sint = _rope_tables(positions)
    dil_tab = _dilated_bias_table()
    fnorm = final_norm.reshape(1, D)

    xf = x.reshape(B * T, D)
    for i in range(depth):
        xf = _ffn(xf, ffn1_norm[i].reshape(1, D), ffn1_w_gate[i].astype(BF16),
                  ffn1_w_up[i].astype(BF16), ffn1_w_down[i].astype(BF16), fnorm, False)

        fq, fk, fv, ffw, dq, dk, dv, hq, hf, hi, hg = _split_w_in(w_in[i])
        ffw_pad = jnp.pad(ffw, ((0, 0), (0, FF_LANES - N_HEADS)))
        wa = jnp.concatenate([fk, dk, _rot_cols(dk), ffw_pad, hq, hf, hi, hg], axis=1).astype(BF16)
        wbt = jnp.concatenate([fq * scale, fv, dq * scale, _rot_cols(dq) * scale, dv],
                              axis=1).T.astype(BF16)
        fb = jnp.pad(fox_forget_bias[i].astype(F32), (0, FF_LANES - N_HEADS)).reshape(1, FF_LANES)
        (fk_a, lf_a, dk_a, hq_a, hl_a, hk_a, hv_a, hs_a, fqt_a, fvt_a, dqt_a, dvt_a) = _inproj(
            xf.reshape(B, T, D), mix_norm[i].reshape(1, D), wa, wbt, cos, sin, cost, sint, fb,
            lbs[i].reshape(1, HG_W))

        oa = _attention(True, fqt_a, fk_a, fvt_a, lf_a)
        ob = _attention(False, dqt_a, dk_a, dvt_a, dil_tab)
        oc = _hgrn(hq_a, hl_a, hk_a, hv_a, hs_a, hgrn_out_norm[i].astype(F32).reshape(1, HG_W))
        xf = _outproj(xf, oa.reshape(B * T, ATT_W), ob.reshape(B * T, ATT_W),
                      oc.reshape(B * T, HG_W), w_out[i].astype(BF16))

        xf = _ffn(xf, ffn2_norm[i].reshape(1, D), ffn2_w_gate[i].astype(BF16),
                  ffn2_w_up[i].astype(BF16), ffn2_w_down[i].astype(BF16), fnorm,
                  i == depth - 1)
    return xf.reshape(B, T, D)
```

```python
import functools
import math

import numpy as np
import jax
import jax.numpy as jnp
from jax import lax
from jax.experimental import pallas as pl
from jax.experimental.pallas import tpu as pltpu

F32 = jnp.float32
BF16 = jnp.bfloat16

D_MODEL = 1024
HEAD_DIM = 64
N_HEADS = 4
ATT_W = N_HEADS * HEAD_DIM
HG_HEADS = 4
HG_E = 128
HG_V = 128
HG_W = HG_HEADS * HG_V
ROPE_DIM = HEAD_DIM // 4
ROPE_THETA = 500000.0
D_FF = ((8 * D_MODEL // 3 + 127) // 128) * 128
EPS = 1e-6
NEG_BIG = -1e30
LB_FLOOR = 1e-30
DILATED_PATTERNS = ((128, 1), (512, 4), (2048, 16))
FF_LANES = 128
VT_ROWS = HEAD_DIM + 16
LOG2E = 1.0 / math.log(2.0)
QK_SCALE = HEAD_DIM ** -0.5 * LOG2E

V7X_VMEM_BYTES = 64 * 1024 * 1024
VMEM_LIMIT = V7X_VMEM_BYTES - 8 * 1024 * 1024
SUBLANES = 8

TM_FFN = 512
FF_CHUNK = 512
TM_PROJ = 512
BQ = 512
BK = 256
HG_CHUNK = 64
HG_SUB = SUBLANES
HG_TB = 1024
CS_BLK = 256


def _rms(x, g):
    ms = jnp.mean(x * x, axis=-1, keepdims=True)
    return x * lax.rsqrt(ms + EPS) * g


def _sigmoid(x):
    return 1.0 / (1.0 + jnp.exp(-x))


def _params(sem):
    return pltpu.CompilerParams(dimension_semantics=sem, vmem_limit_bytes=VMEM_LIMIT)


def _swiglu_residual(x, g_ref, wg_ref, wu_ref, wd_ref, act_ref):
    h = _rms(x, g_ref[...]).astype(BF16)
    for c in range(0, D_FF, FF_CHUNK):
        w = min(FF_CHUNK, D_FF - c)
        g = jnp.dot(h, wg_ref[:, c:c + w], preferred_element_type=F32)
        u = jnp.dot(h, wu_ref[:, c:c + w], preferred_element_type=F32)
        act_ref[:, c:c + w] = (g * _sigmoid(g) * u).astype(BF16)
    return x + 0.5 * jnp.dot(act_ref[...], wd_ref[...], preferred_element_type=F32)


def _ffn_kernel(x_ref, g_ref, wg_ref, wu_ref, wd_ref, o_ref, act_ref):
    o_ref[...] = _swiglu_residual(x_ref[...], g_ref, wg_ref, wu_ref, wd_ref, act_ref)


def _mix_ffn_kernel(final, x_ref, oa_ref, ob_ref, oc_ref, wo_ref, g_ref, wg_ref, wu_ref, wd_ref,
                    fg_ref, o_ref, act_ref):
    W = ATT_W
    x = x_ref[...] + jnp.dot(oa_ref[...], wo_ref[0:W, :], preferred_element_type=F32)
    x = x + jnp.dot(ob_ref[...], wo_ref[W:2 * W, :], preferred_element_type=F32)
    x = x + jnp.dot(oc_ref[...], wo_ref[2 * W:, :], preferred_element_type=F32)
    y = _swiglu_residual(x, g_ref, wg_ref, wu_ref, wd_ref, act_ref)
    if final:
        y = _rms(y, fg_ref[...])
    o_ref[...] = y


def _layer_spec(arr, layer):
    return pl.BlockSpec((None,) + arr.shape[1:], lambda i: (layer, 0, 0),
                        pipeline_mode=pl.Buffered(1))


def _ffn(x2d, layer, g, wg, wu, wd):
    n = x2d.shape[0]
    row = lambda i: (i, 0)
    return pl.pallas_call(
        _ffn_kernel,
        grid=(n // TM_FFN,),
        in_specs=[pl.BlockSpec((TM_FFN, D_MODEL), row), _layer_spec(g, layer),
                  _layer_spec(wg, layer), _layer_spec(wu, layer), _layer_spec(wd, layer)],
        out_specs=pl.BlockSpec((TM_FFN, D_MODEL), row),
        out_shape=jax.ShapeDtypeStruct((n, D_MODEL), F32),
        scratch_shapes=[pltpu.VMEM((TM_FFN, D_FF), BF16)],
        compiler_params=_params(("parallel",)),
        name="ffn",
    )(x2d, g, wg, wu, wd)


def _mix_ffn(x2d, oa, ob, oc, layer, wo, g, wg, wu, wd, fg, final):
    n = x2d.shape[0]
    row = lambda i: (i, 0)
    return pl.pallas_call(
        functools.partial(_mix_ffn_kernel, final),
        grid=(n // TM_FFN,),
        in_specs=[pl.BlockSpec((TM_FFN, D_MODEL), row),
                  pl.BlockSpec((TM_FFN, ATT_W), row),
                  pl.BlockSpec((TM_FFN, ATT_W), row),
                  pl.BlockSpec((TM_FFN, HG_W), row),
                  _layer_spec(wo, layer), _layer_spec(g, layer),
                  _layer_spec(wg, layer), _layer_spec(wu, layer), _layer_spec(wd, layer),
                  pl.BlockSpec((1, D_MODEL), lambda i: (0, 0))],
        out_specs=pl.BlockSpec((TM_FFN, D_MODEL), row),
        out_shape=jax.ShapeDtypeStruct((n, D_MODEL), F32),
        scratch_shapes=[pltpu.VMEM((TM_FFN, D_FF), BF16)],
        compiler_params=_params(("parallel",)),
        name="mix_ffn_final" if final else "mix_ffn",
    )(x2d, oa, ob, oc, wo, g, wg, wu, wd, fg)


def _inproj_kernel(x_ref, g_ref, wa_ref, wbt_ref, cos_ref, sin_ref, cost_ref, sint_ref, fb_ref,
                   lb_ref,
                   fk_ref, lf_ref, dk_ref, hq_ref, hl_ref, hk_ref, hv_ref, hs_ref,
                   fqt_ref, fvt_ref, dqt_ref, dvt_ref):
    h = _rms(x_ref[0], g_ref[...]).astype(BF16)
    W = ATT_W
    pa = jnp.dot(h, wa_ref[:, 0:3 * W + FF_LANES], preferred_element_type=F32)
    fk_ref[0] = pa[:, 0:W].astype(BF16)
    dk_ref[0] = (pa[:, W:2 * W] * cos_ref[0] + pa[:, 2 * W:3 * W] * sin_ref[0]).astype(BF16)
    z = pa[:, 3 * W:3 * W + FF_LANES] + fb_ref[...]
    lf_ref[0] = (jnp.minimum(z, 0.0) - jnp.log1p(jnp.exp(-jnp.abs(z)))) * LOG2E

    base = 3 * W + FF_LANES
    hdot = lambda i: jnp.dot(h, wa_ref[:, base + i * HG_W:base + (i + 1) * HG_W],
                             preferred_element_type=F32)
    q = hdot(0)
    hq_ref[0] = q * _sigmoid(q)
    z = hdot(1)
    lbf = jnp.clip(lb_ref[...], 0.0, 1.0 - 1e-6)
    ez = jnp.exp(-jnp.abs(z))
    inv = 1.0 / (1.0 + ez)
    sig_pos = jnp.where(z >= 0, inv, ez * inv)
    sig_neg = jnp.where(z >= 0, ez * inv, inv)
    hl_ref[0] = jnp.log(jnp.maximum(lbf, LB_FLOOR) + (1.0 - lbf) * sig_pos) * LOG2E
    hk_ref[0] = (1.0 - lbf) * sig_neg
    hv_ref[0] = hdot(2)
    hs_ref[0] = _sigmoid(hdot(3))

    pb = lax.dot_general(wbt_ref[...], h, (((1,), (1,)), ((), ())),
                         preferred_element_type=F32)
    fqt_ref[0] = (pb[0:W] * QK_SCALE).astype(BF16)
    dqt_ref[0] = ((pb[2 * W:3 * W] * cost_ref[0] + pb[3 * W:4 * W] * sint_ref[0]) * QK_SCALE).astype(BF16)
    ones = jnp.ones((VT_ROWS - HEAD_DIM, BK), BF16)
    for src, ref in ((pb[W:2 * W].astype(BF16), fvt_ref), (pb[4 * W:5 * W].astype(BF16), dvt_ref)):
        for j in range(TM_PROJ // BK):
            ref[0, j] = jnp.concatenate(
                [part for h in range(N_HEADS)
                 for part in (src[h * HEAD_DIM:(h + 1) * HEAD_DIM, j * BK:(j + 1) * BK], ones)],
                axis=0)


def _inproj(x, layer, g, wa, wbt, cos, sin, cost, sint, fb, lb):
    B, T, _ = x.shape
    W = ATT_W
    nt = T // TM_PROJ
    per_layer = lambda arr: pl.BlockSpec((None,) + arr.shape[1:], lambda b, t: (layer, 0, 0),
                                         pipeline_mode=pl.Buffered(1))
    row = lambda b, t: (b, t, 0)
    col = lambda b, t: (b, 0, t)
    blk4 = lambda b, t: (b, t, 0, 0)
    hg_shape = jax.ShapeDtypeStruct((B, T, HG_W), F32)
    hg_spec = pl.BlockSpec((1, TM_PROJ, HG_W), row)
    out_shape = [
        jax.ShapeDtypeStruct((B, T, W), BF16),
        jax.ShapeDtypeStruct((B, T, FF_LANES), F32),
        jax.ShapeDtypeStruct((B, T, W), BF16),
        hg_shape, hg_shape, hg_shape, hg_shape, hg_shape,
        jax.ShapeDtypeStruct((B, W, T), BF16),
        jax.ShapeDtypeStruct((B, T // BK, N_HEADS * VT_ROWS, BK), BF16),
        jax.ShapeDtypeStruct((B, W, T), BF16),
        jax.ShapeDtypeStruct((B, T // BK, N_HEADS * VT_ROWS, BK), BF16),
    ]
    out_specs = [
        pl.BlockSpec((1, TM_PROJ, W), row),
        pl.BlockSpec((1, TM_PROJ, FF_LANES), row),
        pl.BlockSpec((1, TM_PROJ, W), row),
        hg_spec, hg_spec, hg_spec, hg_spec, hg_spec,
        pl.BlockSpec((1, W, TM_PROJ), col),
        pl.BlockSpec((1, TM_PROJ // BK, N_HEADS * VT_ROWS, BK), blk4),
        pl.BlockSpec((1, W, TM_PROJ), col),
        pl.BlockSpec((1, TM_PROJ // BK, N_HEADS * VT_ROWS, BK), blk4),
    ]
    return pl.pallas_call(
        _inproj_kernel,
        grid=(B, nt),
        in_specs=[
            pl.BlockSpec((1, TM_PROJ, D_MODEL), row),
            per_layer(g), per_layer(wa), per_layer(wbt),
            pl.BlockSpec((1, TM_PROJ, W), row),
            pl.BlockSpec((1, TM_PROJ, W), row),
            pl.BlockSpec((1, W, TM_PROJ), col),
            pl.BlockSpec((1, W, TM_PROJ), col),
            per_layer(fb), per_layer(lb),
        ],
        out_specs=out_specs,
        out_shape=out_shape,
        compiler_params=_params(("parallel", "parallel")),
        name="inproj",
    )(x, g, wa, wbt, cos, sin, cost, sint, fb, lb)


def _attn_kernel(fox, qt_ref, k_ref, vt_ref, aux_ref, o_ref, qm_ref, acc_ref, st0_ref, st1_ref,
                 *scratch):
    qi = pl.program_id(1)
    T = k_ref.shape[1]
    R = BQ // BK
    assert R == 2
    qt = qt_ref[0]

    if fox:
        ka_ref, = scratch
        n_split = 3

        @pl.when(qi == 0)
        def _():
            r = lax.broadcasted_iota(jnp.int32, (CS_BLK, CS_BLK), 0)
            c = lax.broadcasted_iota(jnp.int32, (CS_BLK, CS_BLK), 1)
            tril = (c <= r).astype(F32)
            lane = lax.broadcasted_iota(jnp.int32, (CS_BLK, HEAD_DIM), 1)
            carry = jnp.zeros((1, FF_LANES), F32)
            for i in range(T // CS_BLK):
                rows = slice(i * CS_BLK, (i + 1) * CS_BLK)
                blk = jnp.dot(tril, aux_ref[0, rows, :], precision=lax.Precision.HIGHEST,
                              preferred_element_type=F32) + carry
                carry = blk[CS_BLK - 1:CS_BLK, :]
                for h in range(N_HEADS):
                    rest = jnp.broadcast_to(blk[:, h:h + 1], (CS_BLK, HEAD_DIM))
                    extra = jnp.zeros((CS_BLK, HEAD_DIM), F32)
                    for piece in range(n_split):
                        part = rest.astype(BF16).astype(F32)
                        extra = jnp.where(lane == piece, part, extra)
                        rest = rest - part
                    ka_ref[rows, 2 * h * HEAD_DIM:(2 * h + 1) * HEAD_DIM] = (
                        k_ref[0, rows, h * HEAD_DIM:(h + 1) * HEAD_DIM])
                    ka_ref[rows, (2 * h + 1) * HEAD_DIM:(2 * h + 2) * HEAD_DIM] = extra.astype(BF16)

        pad_row = lax.broadcasted_iota(jnp.int32, (HEAD_DIM, BQ), 0)
        minus_ones = jnp.where(pad_row < n_split, -1.0, 0.0).astype(BF16)
        for h in range(N_HEADS):
            qm_ref[h] = jnp.concatenate([qt[h * HEAD_DIM:(h + 1) * HEAD_DIM], minus_ones], axis=0)
    else:
        head_of_row = lax.broadcasted_iota(jnp.int32, qt.shape, 0) // HEAD_DIM
        for h in range(N_HEADS):
            qm_ref[h] = jnp.where(head_of_row == h, qt, jnp.zeros_like(qt))
    acc_ref[...] = jnp.zeros_like(acc_ref)
    krow = lax.broadcasted_iota(jnp.int32, (BK, BQ), 0)
    qcol = lax.broadcasted_iota(jnp.int32, (BK, BQ), 1)

    st_slots = (st0_ref, st1_ref)

    def fetch(j):
        start = pl.multiple_of(j * BK, BK)
        if fox:
            kbs = [ka_ref[pl.ds(start, BK), 2 * h * HEAD_DIM:2 * (h + 1) * HEAD_DIM]
                   for h in range(N_HEADS)]
            aux = None
        else:
            kbs = [k_ref[0, pl.ds(start, BK), :]] * N_HEADS
            aux = jnp.concatenate([aux_ref[R * qi + half - j + 1] for half in range(R)], axis=1)
        vbs = [vt_ref[0, j, h * VT_ROWS:(h + 1) * VT_ROWS, :] for h in range(N_HEADS)]
        return kbs, vbs, aux

    def stage(blk, slot, ms, nxt=None, diag=None):
        _, vbs, aux = blk
        new_ms = []
        for h in range(N_HEADS):
            if nxt is not None:
                st_slots[1 - slot][h] = jnp.dot(nxt[0][h], qm_ref[h], preferred_element_type=F32)
            st = st_slots[slot][h]
            if not fox:
                st = st + aux
            elif diag is not None:
                st = jnp.where(krow + diag * BK <= qcol, st, NEG_BIG)
            m_new = jnp.maximum(ms[h], jnp.max(st, axis=0, keepdims=True))
            p = jnp.exp2(st - m_new)
            alpha = jnp.exp2(ms[h] - m_new)
            new_ms.append(m_new)
            rows = slice(h * VT_ROWS, (h + 1) * VT_ROWS)
            acc_ref[rows, :] = alpha * acc_ref[rows, :] + jnp.dot(
                vbs[h], p.astype(BF16), preferred_element_type=F32)
        return tuple(new_ms)

    def pair(i, ms):
        b0, b1, b2 = fetch(2 * i), fetch(2 * i + 1), fetch(2 * i + 2)
        ms = stage(b0, 0, ms, nxt=b1)
        return stage(b1, 1, ms, nxt=b2)

    ms = tuple(jnp.full((1, BQ), NEG_BIG, F32) for _ in range(N_HEADS))
    first = fetch(0)
    for h in range(N_HEADS):
        st_slots[0][h] = jnp.dot(first[0][h], qm_ref[h], preferred_element_type=F32)
    ms = lax.fori_loop(0, qi, pair, ms)
    b0, b1 = fetch(R * qi), fetch(R * qi + 1)
    ms = stage(b0, 0, ms, nxt=b1, diag=0)
    stage(b1, 1, ms, diag=1)
    outs = []
    for h in range(N_HEADS):
        num = acc_ref[h * VT_ROWS:h * VT_ROWS + HEAD_DIM, :]
        den = acc_ref[h * VT_ROWS + HEAD_DIM:h * VT_ROWS + HEAD_DIM + 1, :]
        outs.append(num / den)
    o_ref[0] = jnp.concatenate(outs, axis=0).T.astype(BF16)


def _attention(fox, qt, k, vt, aux):
    B, T, W = k.shape
    nq = T // BQ
    scratch = [pltpu.VMEM((N_HEADS, 2 * HEAD_DIM if fox else W, BQ), BF16),
               pltpu.VMEM((N_HEADS * VT_ROWS, BQ), F32),
               pltpu.VMEM((N_HEADS, BK, BQ), F32), pltpu.VMEM((N_HEADS, BK, BQ), F32)]
    if fox:
        aux_spec = pl.BlockSpec((1, T, FF_LANES), lambda b, q: (b, 0, 0))
        scratch.append(pltpu.VMEM((T, N_HEADS * 2 * HEAD_DIM), BF16))
    else:
        aux_spec = pl.BlockSpec(aux.shape, lambda b, q: (0, 0, 0))
    return pl.pallas_call(
        functools.partial(_attn_kernel, fox),
        grid=(B, nq),
        in_specs=[
            pl.BlockSpec((1, W, BQ), lambda b, q: (b, 0, q)),
            pl.BlockSpec((1, T, W), lambda b, q: (b, 0, 0)),
            pl.BlockSpec((1, T // BK, N_HEADS * VT_ROWS, BK), lambda b, q: (b, 0, 0, 0)),
            aux_spec,
        ],
        out_specs=pl.BlockSpec((1, BQ, W), lambda b, q: (b, q, 0)),
        out_shape=jax.ShapeDtypeStruct((B, T, W), BF16),
        scratch_shapes=scratch,
        compiler_params=_params(("parallel", "arbitrary")),
        name="fox_attention" if fox else "dilated_attention",
    )(qt, k, vt, aux)


def _dilated_bias_table():
    nd = max(w for w, _ in DILATED_PATTERNS) // BK
    d = ((np.arange(nd + 1)[:, None, None] - 1) * BK + np.arange(BK)[None, None, :]
         - np.arange(BK)[None, :, None])
    mult = np.zeros(d.shape, np.float64)
    for window, dil in DILATED_PATTERNS:
        mult += (d >= 0) & (d <= window) & (d % dil == 0)
    tab = np.where(mult > 0, np.log2(np.maximum(mult, 1.0)), NEG_BIG)
    return jnp.asarray(tab, F32)


def _hgrn_kernel(q_ref, lf_ref, k_ref, v_ref, s_ref, nw_ref, o_ref,
                 st_ref, b_scr, k_scr, a_scr, g_scr, bn_scr, ta_scr, ti_scr, tv_scr, tg_scr):
    C, S, E = HG_CHUNK, HG_SUB, HG_E
    NS = C // S
    n_chunks = HG_TB // C
    heads = range(HG_HEADS)
    lanes = [slice(hd * E, (hd + 1) * E) for hd in heads]

    @pl.when(pl.program_id(1) == 0)
    def _():
        st_ref[...] = jnp.zeros_like(st_ref)

    r = lax.broadcasted_iota(jnp.int32, (C, C), 0)
    c = lax.broadcasted_iota(jnp.int32, (C, C), 1)
    tril = (c <= r).astype(F32)
    diag_mask = ((c // S) == (r // S)) & (c <= r)
    rr = lax.broadcasted_iota(jnp.int32, (S * E, C), 0)
    rc = lax.broadcasted_iota(jnp.int32, (S * E, C), 1)
    red = jnp.where(rr // E == rc % S, 1.0, 0.0).astype(BF16)
    first_row = lax.broadcasted_iota(jnp.int32, (NS, E), 0) == 0
    zeros_sub = jnp.zeros((S, E), F32)

    def rows_bcast(ref, hd, idx):
        return jnp.concatenate(
            [jnp.broadcast_to(ref[hd, idx(i):idx(i) + 1, :], (S, E)) for i in range(NS)], axis=0)

    def cumsum(ci):
        r0 = pl.multiple_of(ci * C, C)
        return [jnp.dot(tril, lf_ref[0, pl.ds(r0, C), lanes[hd]], precision=lax.Precision.HIGHEST,
                        preferred_element_type=F32) for hd in heads]

    def out_dots():
        return [ti_scr[hd] + jnp.dot(ta_scr[hd], tv_scr[hd], preferred_element_type=F32)
                for hd in heads]

    def out_finish(ci, os_, gates):
        r0 = pl.multiple_of(ci * C, C)
        for hd in heads:
            o = os_[hd]
            o = o * lax.rsqrt(jnp.mean(o * o, axis=-1, keepdims=True) + EPS)
            o_ref[0, pl.ds(r0, C), lanes[hd]] = (o * gates[hd]).astype(BF16)

    def scores(ci, bs_):
        r0 = pl.multiple_of(ci * C, C)
        for hd in heads:
            b = bs_[hd]
            q = q_ref[0, pl.ds(r0, C), lanes[hd]]
            k = k_ref[0, pl.ds(r0, C), lanes[hd]]
            v = v_ref[0, pl.ds(r0, C), lanes[hd]]
            b_last = b[C - 1:C, :]
            st = st_ref[hd]
            qe = (q * jnp.exp2(b)).astype(BF16)
            ti_scr[hd] = lax.dot_general(qe, st.astype(BF16), (((1,), (1,)), ((), ())),
                                         preferred_element_type=F32)
            kdec = (k * jnp.exp2(b_last - b)).astype(BF16)
            st_ref[hd] = st * jnp.exp2(b_last) + jnp.dot(
                v.T.astype(BF16), kdec, preferred_element_type=F32)
            tv_scr[hd] = v.astype(BF16)
            tg_scr[hd] = s_ref[0, pl.ds(r0, C), lanes[hd]] * nw_ref[:, lanes[hd]]
            b_scr[hd] = b
            k_scr[hd] = k
            b_end = b_scr[hd, pl.ds(S - 1, NS, stride=S), :]
            b_prev = jnp.where(first_row, 0.0, pltpu.roll(b_end, 1, axis=0))
            a_scr[hd, 0:NS] = b_end
            a_scr[hd, NS:2 * NS] = b_prev
            q_t = q * jnp.exp2(b - rows_bcast(a_scr, hd, lambda i: NS + i))
            k_h = k * jnp.exp2(rows_bcast(a_scr, hd, lambda i: i) - b)
            for j in range(NS - 1):
                g_scr[hd, j] = jnp.exp2(jnp.minimum(b_prev - b_end[j:j + 1, :], 0.0))
            lhs_rows, rhs_rows = [], []
            for i in range(NS):
                qi_t = q_t[i * S:(i + 1) * S, :]
                lhs_rows.append(jnp.concatenate(
                    [qi_t * g_scr[hd, j, i:i + 1, :] if i > j else zeros_sub
                     for j in range(NS - 1)], axis=1))
                rhs_rows.append(jnp.concatenate(
                    [k_h[i * S:(i + 1) * S, :] if i == j else zeros_sub
                     for j in range(NS - 1)], axis=1))
            a_off = lax.dot_general(jnp.concatenate(lhs_rows, axis=0).astype(BF16),
                                    jnp.concatenate(rhs_rows, axis=0).astype(BF16),
                                    (((1,), (1,)), ((), ())),
                                    preferred_element_type=F32)
            w_parts = []
            for s in range(S):
                bs = rows_bcast(b_scr, hd, lambda i: i * S + s)
                ks = rows_bcast(k_scr, hd, lambda i: i * S + s)
                w_parts.append((jnp.exp2(jnp.minimum(b - bs, 0.0)) * (q * ks)).astype(BF16))
            a_diag = jnp.dot(jnp.concatenate(w_parts, axis=1), red,
                             preferred_element_type=F32)
            ta_scr[hd] = (a_off + jnp.where(diag_mask, a_diag, 0.0)).astype(BF16)

    def step(ci, _):
        bs_ = [bn_scr[hd] for hd in heads]
        gates = [tg_scr[hd] for hd in heads]
        nxt = cumsum(jnp.minimum(ci + 1, n_chunks - 1))
        os_ = out_dots()
        scores(ci, bs_)
        out_finish(jnp.maximum(ci - 1, 0), os_, gates)
        for hd in heads:
            bn_scr[hd] = nxt[hd]
        return 0

    for ref in (ta_scr, ti_scr, tv_scr, tg_scr):
        ref[...] = jnp.zeros_like(ref)
    first = cumsum(0)
    for hd in heads:
        bn_scr[hd] = first[hd]
    lax.fori_loop(0, n_chunks, step, 0)
    out_finish(n_chunks - 1, out_dots(), [tg_scr[hd] for hd in heads])


def _hgrn(hq, hl, hk, hv, hs, layer, nw):
    B, T, W = hq.shape
    row = lambda b, t: (b, t, 0)
    blk = pl.BlockSpec((1, HG_TB, W), row)
    C, NS = HG_CHUNK, HG_CHUNK // HG_SUB
    per_head = lambda *shape, dtype=F32: pltpu.VMEM((HG_HEADS,) + shape, dtype)
    return pl.pallas_call(
        _hgrn_kernel,
        grid=(B, T // HG_TB),
        in_specs=[blk, blk, blk, blk, blk,
                  pl.BlockSpec((None, 1, W), lambda b, t: (layer, 0, 0))],
        out_specs=pl.BlockSpec((1, HG_TB, W), row),
        out_shape=jax.ShapeDtypeStruct((B, T, W), BF16),
        scratch_shapes=[
            per_head(HG_V, HG_E),
            per_head(C, HG_E),
            per_head(C, HG_E),
            per_head(2 * NS, HG_E),
            per_head(NS - 1, NS, HG_E),
            per_head(C, HG_E),
            per_head(C, C, dtype=BF16),
            per_head(C, HG_V),
            per_head(C, HG_V, dtype=BF16),
            per_head(C, HG_V),
        ],
        compiler_params=_params(("parallel", "arbitrary")),
        name="hgrn2",
    )(hq, hl, hk, hv, hs, nw)


def _rot_cols(w):
    half = ROPE_DIM // 2
    w4 = w.reshape(w.shape[:-1] + (N_HEADS, HEAD_DIM))
    rot = jnp.concatenate([-w4[..., half:ROPE_DIM], w4[..., :half],
                           jnp.zeros_like(w4[..., ROPE_DIM:])], axis=-1)
    return rot.reshape(w.shape)


def _rope_tables(positions):
    freqs = ROPE_THETA ** (-jnp.arange(0, ROPE_DIM, 2, dtype=F32) / ROPE_DIM)
    ang = positions.astype(F32)[:, :, None] * freqs
    B, T = positions.shape
    ones = jnp.ones((B, T, HEAD_DIM - ROPE_DIM), F32)
    cos_h = jnp.concatenate([jnp.cos(ang), jnp.cos(ang), ones], axis=-1)
    sin_h = jnp.concatenate([jnp.sin(ang), jnp.sin(ang), 0.0 * ones], axis=-1)
    cos = jnp.tile(cos_h, (1, 1, N_HEADS))
    sin = jnp.tile(sin_h, (1, 1, N_HEADS))
    return cos, sin, cos.transpose(0, 2, 1), sin.transpose(0, 2, 1)


def _split_w_in(w):
    W = ATT_W
    sizes = (W, W, W, N_HEADS, W, W, W, HG_W, HG_W, HG_W, HG_W)
    pts = tuple(int(s) for s in np.cumsum(sizes)[:-1])
    return jnp.split(w, pts, axis=-1)


def kernel(x, positions, ffn1_norm, ffn1_w_gate, ffn1_w_up, ffn1_w_down, mix_norm, w_in,
           fox_forget_bias, hgrn_lower_bounds, hgrn_out_norm, w_out, ffn2_norm,
           ffn2_w_gate, ffn2_w_up, ffn2_w_down, final_norm):
    B, T, D = x.shape
    depth = w_in.shape[0]
    assert D == D_MODEL and T % max(TM_PROJ, HG_TB, BQ) == 0 and (B * T) % TM_FFN == 0
    assert BQ % BK == 0 and TM_PROJ % BK == 0

    sm = jax.nn.softmax(hgrn_lower_bounds.astype(F32), axis=0)
    lbs = (jnp.cumsum(sm, axis=0) - sm[0:1]).reshape(depth, 1, HG_W)
    fq, fk, fv, ffw, dq, dk, dv, hq, hf, hi, hg = _split_w_in(w_in.astype(BF16))
    ffw_pad = jnp.pad(ffw, ((0, 0), (0, 0), (0, FF_LANES - N_HEADS)))
    wa = jnp.concatenate([fk, dk, _rot_cols(dk), ffw_pad, hq, hf, hi, hg], axis=-1)
    wbt = jnp.concatenate([fq, fv, dq, _rot_cols(dq), dv], axis=-1).transpose(0, 2, 1)
    fb = jnp.pad(fox_forget_bias.astype(F32), ((0, 0), (0, FF_LANES - N_HEADS)))
    fb = fb.reshape(depth, 1, FF_LANES)
    stack_row = lambda p: p.astype(F32).reshape(depth, 1, p.shape[-1])
    n1, nm, n2, nh = (stack_row(p) for p in (ffn1_norm, mix_norm, ffn2_norm, hgrn_out_norm))
    w1 = [w.astype(BF16) for w in (ffn1_w_gate, ffn1_w_up, ffn1_w_down)]
    w2 = [w.astype(BF16) for w in (ffn2_w_gate, ffn2_w_up, ffn2_w_down)]
    wo = w_out.astype(BF16)
    fnorm = final_norm.astype(F32).reshape(1, D)

    cos, sin, cost, sint = _rope_tables(positions)
    dil_tab = _dilated_bias_table()

    xf = x.reshape(B * T, D)
    for i in range(depth):
        xf = _ffn(xf, i, n1, *w1)
        (fk_a, lf_a, dk_a, hq_a, hl_a, hk_a, hv_a, hs_a, fqt_a, fvt_a, dqt_a, dvt_a) = _inproj(
            xf.reshape(B, T, D), i, nm, wa, wbt, cos, sin, cost, sint, fb, lbs)
        oa = _attention(True, fqt_a, fk_a, fvt_a, lf_a)
        ob = _attention(False, dqt_a, dk_a, dvt_a, dil_tab)
        oc = _hgrn(hq_a, hl_a, hk_a, hv_a, hs_a, i, nh)
        xf = _mix_ffn(xf, oa.reshape(B * T, ATT_W), ob.reshape(B * T, ATT_W),
                      oc.reshape(B * T, HG_W), i, wo, n2, *w2, fnorm, i == depth - 1)
    return xf.reshape(B, T, D)
```

```python
import functools
import math

import numpy as np
import jax
import jax.numpy as jnp
from jax import lax
from jax.experimental import pallas as pl
from jax.experimental.pallas import tpu as pltpu

F32 = jnp.float32
BF16 = jnp.bfloat16

D_MODEL = 1024
HEAD_DIM = 64
N_HEADS = 4
ATT_W = N_HEADS * HEAD_DIM
HG_HEADS = 4
HG_E = 128
HG_V = 128
HG_W = HG_HEADS * HG_V
ROPE_DIM = HEAD_DIM // 4
ROPE_THETA = 500000.0
D_FF = ((8 * D_MODEL // 3 + 127) // 128) * 128
EPS = 1e-6
NEG_BIG = -1e30
LB_FLOOR = 1e-30
DILATED_PATTERNS = ((128, 1), (512, 4), (2048, 16))
FF_LANES = 128
VT_ROWS = HEAD_DIM + 16
LOG2E = 1.0 / math.log(2.0)
QK_SCALE = HEAD_DIM ** -0.5 * LOG2E
N_SPLIT = 3

V7X_VMEM_BYTES = 64 * 1024 * 1024
VMEM_LIMIT = V7X_VMEM_BYTES - 8 * 1024 * 1024
SUBLANES = 8
LANES = 128

TM_FFN = 512
FF_CHUNK = 512
TM_PROJ = 512
BQ = 512
BK = 256
HG_CHUNK = 64
HG_SUB = SUBLANES
HG_TB = 1024
CS_BLK = 256


def _rms(x, g):
    ms = jnp.mean(x * x, axis=-1, keepdims=True)
    return x * lax.rsqrt(ms + EPS) * g


def _sigmoid(x):
    return 1.0 / (1.0 + jnp.exp(-x))


def _split3(x):
    hi = x.astype(BF16)
    rest = x - hi.astype(F32)
    mid = rest.astype(BF16)
    lo = (rest - mid.astype(F32)).astype(BF16)
    return hi, mid, lo


def _tril3(n):
    width = -(-N_SPLIT * n // LANES) * LANES
    r = lax.broadcasted_iota(jnp.int32, (n, width), 0)
    c = lax.broadcasted_iota(jnp.int32, (n, width), 1)
    return jnp.where((c % n <= r) & (c < N_SPLIT * n), 1.0, 0.0).astype(BF16)


def _cumsum_rows(tril3, x):
    pieces = list(_split3(x))
    pad = tril3.shape[1] - N_SPLIT * x.shape[0]
    if pad:
        pieces.append(jnp.zeros((pad, x.shape[1]), BF16))
    return jnp.dot(tril3, jnp.concatenate(pieces, axis=0), preferred_element_type=F32)


def _params(sem):
    return pltpu.CompilerParams(dimension_semantics=sem, vmem_limit_bytes=VMEM_LIMIT)


def _swiglu_residual(x, g_ref, wg_ref, wu_ref, wd_ref):
    h = _rms(x, g_ref[...]).astype(BF16)
    acc = jnp.zeros_like(x)
    for c in range(0, D_FF, FF_CHUNK):
        w = min(FF_CHUNK, D_FF - c)
        g = jnp.dot(h, wg_ref[:, c:c + w].astype(BF16), preferred_element_type=F32)
        u = jnp.dot(h, wu_ref[:, c:c + w].astype(BF16), preferred_element_type=F32)
        a = (g * _sigmoid(g) * u).astype(BF16)
        acc = acc + jnp.dot(a, wd_ref[c:c + w, :].astype(BF16), preferred_element_type=F32)
    return x + 0.5 * acc


def _ffn_kernel(x_ref, g_ref, wg_ref, wu_ref, wd_ref, o_ref):
    o_ref[...] = _swiglu_residual(x_ref[...], g_ref, wg_ref, wu_ref, wd_ref)


def _mix_ffn_kernel(final, x_ref, oa_ref, ob_ref, oc_ref, wo_ref, g_ref, wg_ref, wu_ref, wd_ref,
                    fg_ref, o_ref):
    W = ATT_W
    wo = lambda lo, hi: wo_ref[lo:hi, :].astype(BF16)
    x = x_ref[...] + jnp.dot(oa_ref[...], wo(0, W), preferred_element_type=F32)
    x = x + jnp.dot(ob_ref[...], wo(W, 2 * W), preferred_element_type=F32)
    x = x + jnp.dot(oc_ref[...], wo(2 * W, D_MODEL), preferred_element_type=F32)
    y = _swiglu_residual(x, g_ref, wg_ref, wu_ref, wd_ref)
    if final:
        y = _rms(y, fg_ref[...])
    o_ref[...] = y


def _layer_spec(arr, layer):
    return pl.BlockSpec((None,) + arr.shape[1:], lambda i: (layer, 0, 0),
                        pipeline_mode=pl.Buffered(1))


def _ffn(x2d, layer, g, wg, wu, wd):
    n = x2d.shape[0]
    row = lambda i: (i, 0)
    return pl.pallas_call(
        _ffn_kernel,
        grid=(n // TM_FFN,),
        in_specs=[pl.BlockSpec((TM_FFN, D_MODEL), row), _layer_spec(g, layer),
                  _layer_spec(wg, layer), _layer_spec(wu, layer), _layer_spec(wd, layer)],
        out_specs=pl.BlockSpec((TM_FFN, D_MODEL), row),
        out_shape=jax.ShapeDtypeStruct((n, D_MODEL), F32),
        compiler_params=_params(("parallel",)),
        name="ffn",
    )(x2d, g, wg, wu, wd)


def _mix_ffn(x2d, oa, ob, oc, layer, wo, g, wg, wu, wd, fg, final):
    n = x2d.shape[0]
    row = lambda i: (i, 0)
    return pl.pallas_call(
        functools.partial(_mix_ffn_kernel, final),
        grid=(n // TM_FFN,),
        in_specs=[pl.BlockSpec((TM_FFN, D_MODEL), row),
                  pl.BlockSpec((TM_FFN, ATT_W), row),
                  pl.BlockSpec((TM_FFN, ATT_W), row),
                  pl.BlockSpec((TM_FFN, HG_W), row),
                  _layer_spec(wo, layer), _layer_spec(g, layer),
                  _layer_spec(wg, layer), _layer_spec(wu, layer), _layer_spec(wd, layer),
                  pl.BlockSpec((1, D_MODEL), lambda i: (0, 0))],
        out_specs=pl.BlockSpec((TM_FFN, D_MODEL), row),
        out_shape=jax.ShapeDtypeStruct((n, D_MODEL), F32),
        compiler_params=_params(("parallel",)),
        name="mix_ffn_final" if final else "mix_ffn",
    )(x2d, oa, ob, oc, wo, g, wg, wu, wd, fg)


def _inproj_kernel(x_ref, g_ref, wa_ref, wbt_ref, cos_ref, sin_ref, cost_ref, sint_ref, fb_ref,
                   lb_ref,
                   fk_ref, lf_ref, dk_ref, hq_ref, hl_ref, hk_ref, hv_ref, hs_ref,
                   fqt_ref, fvt_ref, dqt_ref, dvt_ref):
    h = _rms(x_ref[0], g_ref[...]).astype(BF16)
    W = ATT_W
    pa = jnp.dot(h, wa_ref[:, 0:3 * W + FF_LANES], preferred_element_type=F32)
    fk_ref[0] = pa[:, 0:W].astype(BF16)
    dk_ref[0] = (pa[:, W:2 * W] * cos_ref[0] + pa[:, 2 * W:3 * W] * sin_ref[0]).astype(BF16)
    z = pa[:, 3 * W:3 * W + FF_LANES] + fb_ref[...]
    lf_ref[0] = (jnp.minimum(z, 0.0) - jnp.log1p(jnp.exp(-jnp.abs(z)))) * LOG2E

    base = 3 * W + FF_LANES
    hdot = lambda i: jnp.dot(h, wa_ref[:, base + i * HG_W:base + (i + 1) * HG_W],
                             preferred_element_type=F32)
    q = hdot(0)
    hq_ref[0] = q * _sigmoid(q)
    z = hdot(1)
    lbf = jnp.clip(lb_ref[...], 0.0, 1.0 - 1e-6)
    ez = jnp.exp(-jnp.abs(z))
    inv = 1.0 / (1.0 + ez)
    sig_pos = jnp.where(z >= 0, inv, ez * inv)
    sig_neg = jnp.where(z >= 0, ez * inv, inv)
    hl_ref[0] = jnp.log(jnp.maximum(lbf, LB_FLOOR) + (1.0 - lbf) * sig_pos) * LOG2E
    hk_ref[0] = (1.0 - lbf) * sig_neg
    hv_ref[0] = hdot(2)
    hs_ref[0] = _sigmoid(hdot(3))

    pb = lax.dot_general(wbt_ref[...], h, (((1,), (1,)), ((), ())),
                         preferred_element_type=F32)
    fqt_ref[0] = (pb[0:W] * QK_SCALE).astype(BF16)
    dqt_ref[0] = ((pb[2 * W:3 * W] * cost_ref[0] + pb[3 * W:4 * W] * sint_ref[0]) * QK_SCALE).astype(BF16)
    ones = jnp.ones((VT_ROWS - HEAD_DIM, BK), BF16)
    for src, ref in ((pb[W:2 * W].astype(BF16), fvt_ref), (pb[4 * W:5 * W].astype(BF16), dvt_ref)):
        for j in range(TM_PROJ // BK):
            ref[0, j] = jnp.concatenate(
                [part for h in range(N_HEADS)
                 for part in (src[h * HEAD_DIM:(h + 1) * HEAD_DIM, j * BK:(j + 1) * BK], ones)],
                axis=0)


def _inproj(x, layer, g, wa, wbt, cos, sin, cost, sint, fb, lb):
    B, T, _ = x.shape
    W = ATT_W
    nt = T // TM_PROJ
    per_layer = lambda arr: pl.BlockSpec((None,) + arr.shape[1:], lambda b, t: (layer, 0, 0),
                                         pipeline_mode=pl.Buffered(1))
    row = lambda b, t: (b, t, 0)
    col = lambda b, t: (b, 0, t)
    blk4 = lambda b, t: (b, t, 0, 0)
    hg_shape = jax.ShapeDtypeStruct((B, T, HG_W), F32)
    hg_spec = pl.BlockSpec((1, TM_PROJ, HG_W), row)
    out_shape = [
        jax.ShapeDtypeStruct((B, T, W), BF16),
        jax.ShapeDtypeStruct((B, T, FF_LANES), F32),
        jax.ShapeDtypeStruct((B, T, W), BF16),
        hg_shape, hg_shape, hg_shape, hg_shape, hg_shape,
        jax.ShapeDtypeStruct((B, W, T), BF16),
        jax.ShapeDtypeStruct((B, T // BK, N_HEADS * VT_ROWS, BK), BF16),
        jax.ShapeDtypeStruct((B, W, T), BF16),
        jax.ShapeDtypeStruct((B, T // BK, N_HEADS * VT_ROWS, BK), BF16),
    ]
    out_specs = [
        pl.BlockSpec((1, TM_PROJ, W), row),
        pl.BlockSpec((1, TM_PROJ, FF_LANES), row),
        pl.BlockSpec((1, TM_PROJ, W), row),
        hg_spec, hg_spec, hg_spec, hg_spec, hg_spec,
        pl.BlockSpec((1, W, TM_PROJ), col),
        pl.BlockSpec((1, TM_PROJ // BK, N_HEADS * VT_ROWS, BK), blk4),
        pl.BlockSpec((1, W, TM_PROJ), col),
        pl.BlockSpec((1, TM_PROJ // BK, N_HEADS * VT_ROWS, BK), blk4),
    ]
    return pl.pallas_call(
        _inproj_kernel,
        grid=(B, nt),
        in_specs=[
            pl.BlockSpec((1, TM_PROJ, D_MODEL), row),
            per_layer(g), per_layer(wa), per_layer(wbt),
            pl.BlockSpec((1, TM_PROJ, W), row),
            pl.BlockSpec((1, TM_PROJ, W), row),
            pl.BlockSpec((1, W, TM_PROJ), col),
            pl.BlockSpec((1, W, TM_PROJ), col),
            per_layer(fb), per_layer(lb),
        ],
        out_specs=out_specs,
        out_shape=out_shape,
        compiler_params=_params(("parallel", "parallel")),
        name="inproj",
    )(x, g, wa, wbt, cos, sin, cost, sint, fb, lb)


def _attn_kernel(fox, qt_ref, k_ref, vt_ref, aux_ref, *rest):
    if fox:
        sel_ref, o_ref, qm_ref, acc_ref, m_ref, st0_ref, st1_ref, ka_ref = rest
    else:
        o_ref, qm_ref, acc_ref, m_ref, st0_ref, st1_ref = rest
    qi = pl.program_id(1)
    T = k_ref.shape[1]
    R = BQ // BK
    assert R == 2
    qt = qt_ref[0]

    if fox:
        @pl.when(qi == 0)
        def _():
            tril3 = _tril3(CS_BLK)
            blocks = [slice(i * CS_BLK, (i + 1) * CS_BLK) for i in range(T // CS_BLK)]
            local = [_cumsum_rows(tril3, aux_ref[0, rows, :]) for rows in blocks]
            carry = jnp.zeros((1, FF_LANES), F32)
            for rows, loc in zip(blocks, local):
                blk = loc + carry
                carry = blk[CS_BLK - 1:CS_BLK, :]
                cat = jnp.concatenate([k_ref[0, rows, :], *_split3(blk)], axis=1)
                ka_ref[rows, :] = jnp.dot(cat, sel_ref[...],
                                          preferred_element_type=F32).astype(BF16)

        pad_row = lax.broadcasted_iota(jnp.int32, (HEAD_DIM, BQ), 0)
        minus_ones = jnp.where(pad_row < N_SPLIT, -1.0, 0.0).astype(BF16)
        for h in range(N_HEADS):
            qm_ref[h] = jnp.concatenate([qt[h * HEAD_DIM:(h + 1) * HEAD_DIM], minus_ones], axis=0)
    else:
        head_of_row = lax.broadcasted_iota(jnp.int32, qt.shape, 0) // HEAD_DIM
        for h in range(N_HEADS):
            qm_ref[h] = jnp.where(head_of_row == h, qt, jnp.zeros_like(qt))
    acc_ref[...] = jnp.zeros_like(acc_ref)
    krow = lax.broadcasted_iota(jnp.int32, (BK, BQ), 0)
    qcol = lax.broadcasted_iota(jnp.int32, (BK, BQ), 1)

    tri = (lax.broadcasted_iota(jnp.int32, (BK, BK), 0)
           <= lax.broadcasted_iota(jnp.int32, (BK, BK), 1))

    st_slots = (st0_ref, st1_ref)
    upper = slice(BK, BQ)

    def fetch(j, last=False):
        start = pl.multiple_of(j * BK, BK)
        if fox:
            kbs = [ka_ref[pl.ds(start, BK), 2 * h * HEAD_DIM:2 * (h + 1) * HEAD_DIM]
                   for h in range(N_HEADS)]
            aux = None
        else:
            kbs = [k_ref[0, pl.ds(start, BK), :]] * N_HEADS
            halves = (1,) if last else range(R)
            aux = jnp.concatenate([aux_ref[R * qi + half - j] for half in halves], axis=1)
        vbs = [vt_ref[0, j, h * VT_ROWS:(h + 1) * VT_ROWS, :] for h in range(N_HEADS)]
        return kbs, vbs, aux

    def stage(blk, slot, nxt=None, nxt_last=False, diag=False, last=False):
        _, vbs, aux = blk
        cols = upper if last else slice(0, BQ)
        nxt_cols = upper if nxt_last else slice(0, BQ)
        for h in range(N_HEADS):
            if nxt is not None:
                st_slots[1 - slot][h, :, nxt_cols] = jnp.dot(
                    nxt[0][h], qm_ref[h, :, nxt_cols], preferred_element_type=F32)
            st = st_slots[slot][h, :, cols]
            if not fox:
                st = st + aux
            elif last:
                st = jnp.where(tri, st, NEG_BIG)
            elif diag:
                st = jnp.where(krow <= qcol, st, NEG_BIG)
            m_old = m_ref[h, :, cols]
            m_new = jnp.maximum(m_old, jnp.max(st, axis=0, keepdims=True))
            m_ref[h, :, cols] = m_new
            p = jnp.exp2(st - m_new)
            alpha = jnp.exp2(m_old - m_new)
            rows = slice(h * VT_ROWS, (h + 1) * VT_ROWS)
            acc_ref[rows, cols] = alpha * acc_ref[rows, cols] + jnp.dot(
                vbs[h], p.astype(BF16), preferred_element_type=F32)

    def pair(i, _):
        b0, b1, b2 = fetch(2 * i), fetch(2 * i + 1), fetch(2 * i + 2)
        stage(b0, 0, nxt=b1)
        stage(b1, 1, nxt=b2)
        return 0

    m_ref[...] = jnp.full(m_ref.shape, NEG_BIG, F32)
    first = fetch(0)
    for h in range(N_HEADS):
        st_slots[0][h] = jnp.dot(first[0][h], qm_ref[h], preferred_element_type=F32)
    lax.fori_loop(0, qi, pair, 0)
    b0, b1 = fetch(R * qi), fetch(R * qi + 1, last=True)
    stage(b0, 0, nxt=b1, nxt_last=True, diag=True)
    stage(b1, 1, last=True)
    outs = []
    for h in range(N_HEADS):
        num = acc_ref[h * VT_ROWS:h * VT_ROWS + HEAD_DIM, :]
        den = acc_ref[h * VT_ROWS + HEAD_DIM:h * VT_ROWS + HEAD_DIM + 1, :]
        outs.append(num / den)
    o_ref[0] = jnp.concatenate(outs, axis=0).T.astype(BF16)


def _fox_slot_matrix():
    sel = np.zeros((ATT_W + N_SPLIT * FF_LANES, N_HEADS * 2 * HEAD_DIM), np.float32)
    for h in range(N_HEADS):
        for d in range(HEAD_DIM):
            sel[h * HEAD_DIM + d, 2 * h * HEAD_DIM + d] = 1.0
        for piece in range(N_SPLIT):
            sel[ATT_W + piece * FF_LANES + h, (2 * h + 1) * HEAD_DIM + piece] = 1.0
    return jnp.asarray(sel, BF16)


def _attention(fox, qt, k, vt, aux):
    B, T, W = k.shape
    nq = T // BQ
    in_specs = [
        pl.BlockSpec((1, W, BQ), lambda b, q: (b, 0, q)),
        pl.BlockSpec((1, T, W), lambda b, q: (b, 0, 0)),
        pl.BlockSpec((1, T // BK, N_HEADS * VT_ROWS, BK), lambda b, q: (b, 0, 0, 0)),
    ]
    scratch = [pltpu.VMEM((N_HEADS, 2 * HEAD_DIM if fox else W, BQ), BF16),
               pltpu.VMEM((N_HEADS * VT_ROWS, BQ), F32),
               pltpu.VMEM((N_HEADS, 1, BQ), F32),
               pltpu.VMEM((N_HEADS, BK, BQ), F32), pltpu.VMEM((N_HEADS, BK, BQ), F32)]
    if fox:
        sel = _fox_slot_matrix()
        operands = (qt, k, vt, aux, sel)
        in_specs += [pl.BlockSpec((1, T, FF_LANES), lambda b, q: (b, 0, 0)),
                     pl.BlockSpec(sel.shape, lambda b, q: (0, 0))]
        scratch.append(pltpu.VMEM((T, N_HEADS * 2 * HEAD_DIM), BF16))
    else:
        operands = (qt, k, vt, aux)
        in_specs.append(pl.BlockSpec(aux.shape, lambda b, q: (0, 0, 0)))
    return pl.pallas_call(
        functools.partial(_attn_kernel, fox),
        grid=(B, nq),
        in_specs=in_specs,
        out_specs=pl.BlockSpec((1, BQ, W), lambda b, q: (b, q, 0)),
        out_shape=jax.ShapeDtypeStruct((B, T, W), BF16),
        scratch_shapes=scratch,
        compiler_params=_params(("parallel", "arbitrary")),
        name="fox_attention" if fox else "dilated_attention",
    )(*operands)


def _dilated_bias_table():
    nd = max(w for w, _ in DILATED_PATTERNS) // BK
    d = (np.arange(nd)[:, None, None] * BK + np.arange(BK)[None, None, :]
         - np.arange(BK)[None, :, None])
    mult = np.zeros(d.shape, np.float64)
    for window, dil in DILATED_PATTERNS:
        mult += (d >= 0) & (d <= window) & (d % dil == 0)
    tab = np.where(mult > 0, np.log2(np.maximum(mult, 1.0)), NEG_BIG)
    return jnp.asarray(tab, F32)


def _hgrn_kernel(q_ref, lf_ref, k_ref, v_ref, s_ref, nw_ref, o_ref,
                 st_ref, b_scr, k_scr, a_scr, g_scr, bn_scr, ta_scr, ti_scr, tv_scr, tg_scr):
    C, S, E = HG_CHUNK, HG_SUB, HG_E
    NS = C // S
    n_chunks = HG_TB // C
    heads = range(HG_HEADS)
    lanes = [slice(hd * E, (hd + 1) * E) for hd in heads]

    @pl.when(pl.program_id(1) == 0)
    def _():
        st_ref[...] = jnp.zeros_like(st_ref)

    r = lax.broadcasted_iota(jnp.int32, (C, C), 0)
    c = lax.broadcasted_iota(jnp.int32, (C, C), 1)
    tril3 = _tril3(C)
    diag_mask = ((c // S) == (r // S)) & (c <= r)
    rr = lax.broadcasted_iota(jnp.int32, (S * E, C), 0)
    rc = lax.broadcasted_iota(jnp.int32, (S * E, C), 1)
    red = jnp.where(rr // E == rc % S, 1.0, 0.0).astype(BF16)
    first_row = lax.broadcasted_iota(jnp.int32, (NS, E), 0) == 0
    zeros_sub = jnp.zeros((S, E), F32)

    def rows_bcast(ref, hd, idx):
        return jnp.concatenate(
            [jnp.broadcast_to(ref[hd, idx(i):idx(i) + 1, :], (S, E)) for i in range(NS)], axis=0)

    def cumsum(ci):
        r0 = pl.multiple_of(ci * C, C)
        return [_cumsum_rows(tril3, lf_ref[0, pl.ds(r0, C), lanes[hd]]) for hd in heads]

    def out_dots():
        return [ti_scr[hd] + jnp.dot(ta_scr[hd], tv_scr[hd], preferred_element_type=F32)
                for hd in heads]

    def out_finish(ci, os_, gates, hd):
        r0 = pl.multiple_of(ci * C, C)
        o = os_[hd]
        o = o * lax.rsqrt(jnp.mean(o * o, axis=-1, keepdims=True) + EPS)
        o_ref[0, pl.ds(r0, C), lanes[hd]] = (o * gates[hd]).astype(BF16)

    def scores(ci, bs_, after_head):
        r0 = pl.multiple_of(ci * C, C)
        for hd in heads:
            b = bs_[hd]
            q = q_ref[0, pl.ds(r0, C), lanes[hd]]
            k = k_ref[0, pl.ds(r0, C), lanes[hd]]
            v = v_ref[0, pl.ds(r0, C), lanes[hd]]
            b_last = b[C - 1:C, :]
            st = st_ref[hd]
            qe = (q * jnp.exp2(b)).astype(BF16)
            ti_scr[hd] = lax.dot_general(qe, st.astype(BF16), (((1,), (1,)), ((), ())),
                                         preferred_element_type=F32)
            kdec = (k * jnp.exp2(b_last - b)).astype(BF16)
            st_ref[hd] = st * jnp.exp2(b_last) + jnp.dot(
                v.T.astype(BF16), kdec, preferred_element_type=F32)
            tv_scr[hd] = v.astype(BF16)
            tg_scr[hd] = s_ref[0, pl.ds(r0, C), lanes[hd]] * nw_ref[:, lanes[hd]]
            b_scr[hd] = b
            k_scr[hd] = k
            b_end = b_scr[hd, pl.ds(S - 1, NS, stride=S), :]
            b_prev = jnp.where(first_row, 0.0, pltpu.roll(b_end, 1, axis=0))
            a_scr[hd, 0:NS] = b_end
            a_scr[hd, NS:2 * NS] = b_prev
            q_t = q * jnp.exp2(b - rows_bcast(a_scr, hd, lambda i: NS + i))
            k_h = k * jnp.exp2(rows_bcast(a_scr, hd, lambda i: i) - b)
            for j in range(NS - 1):
                g_scr[hd, j] = jnp.exp2(jnp.minimum(b_prev - b_end[j:j + 1, :], 0.0))
            lhs_rows, rhs_rows = [], []
            for i in range(NS):
                qi_t = q_t[i * S:(i + 1) * S, :]
                lhs_rows.append(jnp.concatenate(
                    [qi_t * g_scr[hd, j, i:i + 1, :] if i > j else zeros_sub
                     for j in range(NS - 1)], axis=1))
                rhs_rows.append(jnp.concatenate(
                    [k_h[i * S:(i + 1) * S, :] if i == j else zeros_sub
                     for j in range(NS - 1)], axis=1))
            a_off = lax.dot_general(jnp.concatenate(lhs_rows, axis=0).astype(BF16),
                                    jnp.concatenate(rhs_rows, axis=0).astype(BF16),
                                    (((1,), (1,)), ((), ())),
                                    preferred_element_type=F32)
            w_parts = []
            for s in range(S):
                bs = rows_bcast(b_scr, hd, lambda i: i * S + s)
                ks = rows_bcast(k_scr, hd, lambda i: i * S + s)
                w_parts.append((jnp.exp2(jnp.minimum(b - bs, 0.0)) * (q * ks)).astype(BF16))
            a_diag = jnp.dot(jnp.concatenate(w_parts, axis=1), red,
                             preferred_element_type=F32)
            ta_scr[hd] = (a_off + jnp.where(diag_mask, a_diag, 0.0)).astype(BF16)
            after_head(hd)

    def step(ci, _):
        bs_ = [bn_scr[hd] for hd in heads]
        gates = [tg_scr[hd] for hd in heads]
        nxt = cumsum(jnp.minimum(ci + 1, n_chunks - 1))
        os_ = out_dots()
        scores(ci, bs_, functools.partial(out_finish, jnp.maximum(ci - 1, 0), os_, gates))
        for hd in heads:
            bn_scr[hd] = nxt[hd]
        return 0

    for ref in (ta_scr, ti_scr, tv_scr, tg_scr):
        ref[...] = jnp.zeros_like(ref)
    first = cumsum(0)
    for hd in heads:
        bn_scr[hd] = first[hd]
    lax.fori_loop(0, n_chunks, step, 0)
    last_os, last_gates = out_dots(), [tg_scr[hd] for hd in heads]
    for hd in heads:
        out_finish(n_chunks - 1, last_os, last_gates, hd)


def _hgrn(hq, hl, hk, hv, hs, layer, nw):
    B, T, W = hq.shape
    row = lambda b, t: (b, t, 0)
    blk = pl.BlockSpec((1, HG_TB, W), row)
    C, NS = HG_CHUNK, HG_CHUNK // HG_SUB
    per_head = lambda *shape, dtype=F32: pltpu.VMEM((HG_HEADS,) + shape, dtype)
    return pl.pallas_call(
        _hgrn_kernel,
        grid=(B, T // HG_TB),
        in_specs=[blk, blk, blk, blk, blk,
                  pl.BlockSpec((None, 1, W), lambda b, t: (layer, 0, 0))],
        out_specs=pl.BlockSpec((1, HG_TB, W), row),
        out_shape=jax.ShapeDtypeStruct((B, T, W), BF16),
        scratch_shapes=[
            per_head(HG_V, HG_E),
            per_head(C, HG_E),
            per_head(C, HG_E),
            per_head(2 * NS, HG_E),
            per_head(NS - 1, NS, HG_E),
            per_head(C, HG_E),
            per_head(C, C, dtype=BF16),
            per_head(C, HG_V),
            per_head(C, HG_V, dtype=BF16),
            per_head(C, HG_V),
        ],
        compiler_params=_params(("parallel", "arbitrary")),
        name="hgrn2",
    )(hq, hl, hk, hv, hs, nw)


def _rot_cols(w):
    half = ROPE_DIM // 2
    w4 = w.reshape(w.shape[:-1] + (N_HEADS, HEAD_DIM))
    rot = jnp.concatenate([-w4[..., half:ROPE_DIM], w4[..., :half],
                           jnp.zeros_like(w4[..., ROPE_DIM:])], axis=-1)
    return rot.reshape(w.shape)


def _rope_tables(positions):
    freqs = ROPE_THETA ** (-jnp.arange(0, ROPE_DIM, 2, dtype=F32) / ROPE_DIM)
    ang = positions.astype(F32)[:, :, None] * freqs
    B, T = positions.shape
    ones = jnp.ones((B, T, HEAD_DIM - ROPE_DIM), F32)
    cos_h = jnp.concatenate([jnp.cos(ang), jnp.cos(ang), ones], axis=-1)
    sin_h = jnp.concatenate([jnp.sin(ang), jnp.sin(ang), 0.0 * ones], axis=-1)
    cos = jnp.tile(cos_h, (1, 1, N_HEADS))
    sin = jnp.tile(sin_h, (1, 1, N_HEADS))
    return cos, sin, cos.transpose(0, 2, 1), sin.transpose(0, 2, 1)


def _split_w_in(w):
    W = ATT_W
    sizes = (W, W, W, N_HEADS, W, W, W, HG_W, HG_W, HG_W, HG_W)
    pts = tuple(int(s) for s in np.cumsum(sizes)[:-1])
    return jnp.split(w, pts, axis=-1)


def kernel(x, positions, ffn1_norm, ffn1_w_gate, ffn1_w_up, ffn1_w_down, mix_norm, w_in,
           fox_forget_bias, hgrn_lower_bounds, hgrn_out_norm, w_out, ffn2_norm,
           ffn2_w_gate, ffn2_w_up, ffn2_w_down, final_norm):
    B, T, D = x.shape
    depth = w_in.shape[0]
    assert D == D_MODEL and T % max(TM_PROJ, HG_TB, BQ) == 0 and (B * T) % TM_FFN == 0
    assert BQ % BK == 0 and TM_PROJ % BK == 0

    sm = jax.nn.softmax(hgrn_lower_bounds.astype(F32), axis=0)
    lbs = (jnp.cumsum(sm, axis=0) - sm[0:1]).reshape(depth, 1, HG_W)
    fq, fk, fv, ffw, dq, dk, dv, hq, hf, hi, hg = _split_w_in(w_in.astype(BF16))
    ffw_pad = jnp.pad(ffw, ((0, 0), (0, 0), (0, FF_LANES - N_HEADS)))
    wa = jnp.concatenate([fk, dk, _rot_cols(dk), ffw_pad, hq, hf, hi, hg], axis=-1)
    wbt = jnp.concatenate([fq, fv, dq, _rot_cols(dq), dv], axis=-1).transpose(0, 2, 1)
    fb = jnp.pad(fox_forget_bias.astype(F32), ((0, 0), (0, FF_LANES - N_HEADS)))
    fb = fb.reshape(depth, 1, FF_LANES)
    stack_row = lambda p: p.astype(F32).reshape(depth, 1, p.shape[-1])
    n1, nm, n2, nh = (stack_row(p) for p in (ffn1_norm, mix_norm, ffn2_norm, hgrn_out_norm))
    w1 = (ffn1_w_gate, ffn1_w_up, ffn1_w_down)
    w2 = (ffn2_w_gate, ffn2_w_up, ffn2_w_down)
    wo = w_out
    fnorm = final_norm.astype(F32).reshape(1, D)

    cos, sin, cost, sint = _rope_tables(positions)
    dil_tab = _dilated_bias_table()

    xf = x.reshape(B * T, D)
    for i in range(depth):
        xf = _ffn(xf, i, n1, *w1)
        (fk_a, lf_a, dk_a, hq_a, hl_a, hk_a, hv_a, hs_a, fqt_a, fvt_a, dqt_a, dvt_a) = _inproj(
            xf.reshape(B, T, D), i, nm, wa, wbt, cos, sin, cost, sint, fb, lbs)
        oa = _attention(True, fqt_a, fk_a, fvt_a, lf_a)
        ob = _attention(False, dqt_a, dk_a, dvt_a, dil_tab)
        oc = _hgrn(hq_a, hl_a, hk_a, hv_a, hs_a, i, nh)
        xf = _mix_ffn(xf, oa.reshape(B * T, ATT_W), ob.reshape(B * T, ATT_W),
                      oc.reshape(B * T, HG_W), i, wo, n2, *w2, fnorm, i == depth - 1)
    return xf.reshape(B, T, D)
```

```python
import functools
import math

import numpy as np
import jax
import jax.numpy as jnp
from jax import lax
from jax.experimental import pallas as pl
from jax.experimental.pallas import tpu as pltpu

F32 = jnp.float32
BF16 = jnp.bfloat16

D_MODEL = 1024
HEAD_DIM = 64
N_HEADS = 4
ATT_W = N_HEADS * HEAD_DIM
HG_HEADS = 4
HG_E = 128
HG_V = 128
HG_W = HG_HEADS * HG_V
ROPE_DIM = HEAD_DIM // 4
ROPE_THETA = 500000.0
D_FF = ((8 * D_MODEL // 3 + 127) // 128) * 128
EPS = 1e-6
NEG_BIG = -1e30
LB_FLOOR = 1e-30
DILATED_PATTERNS = ((128, 1), (512, 4), (2048, 16))
FF_LANES = 128
VT_ROWS = HEAD_DIM + 16
LOG2E = 1.0 / math.log(2.0)
QK_SCALE = HEAD_DIM ** -0.5 * LOG2E
N_SPLIT = 3

V7X_VMEM_BYTES = 64 * 1024 * 1024
VMEM_LIMIT = V7X_VMEM_BYTES - 8 * 1024 * 1024
SUBLANES = 8
LANES = 128

TM_FFN = 512
FF_CHUNK = 512
TM_PROJ = 512
BQ = 512
BK = 256
HG_CHUNK = 64
HG_SUB = SUBLANES
HG_TB = 1024
CS_BLK = 256


def _rms(x, g):
    ms = jnp.mean(x * x, axis=-1, keepdims=True)
    return x * lax.rsqrt(ms + EPS) * g


def _sigmoid(x):
    return 1.0 / (1.0 + jnp.exp(-x))


def _split3(x):
    hi = x.astype(BF16)
    rest = x - hi.astype(F32)
    mid = rest.astype(BF16)
    lo = (rest - mid.astype(F32)).astype(BF16)
    return hi, mid, lo


def _tril3(n):
    width = -(-N_SPLIT * n // LANES) * LANES
    r = lax.broadcasted_iota(jnp.int32, (n, width), 0)
    c = lax.broadcasted_iota(jnp.int32, (n, width), 1)
    return jnp.where((c % n <= r) & (c < N_SPLIT * n), 1.0, 0.0).astype(BF16)


def _cumsum_rows(tril3, x):
    pieces = list(_split3(x))
    pad = tril3.shape[1] - N_SPLIT * x.shape[0]
    if pad:
        pieces.append(jnp.zeros((pad, x.shape[1]), BF16))
    return jnp.dot(tril3, jnp.concatenate(pieces, axis=0), preferred_element_type=F32)


def _params(sem):
    return pltpu.CompilerParams(dimension_semantics=sem, vmem_limit_bytes=VMEM_LIMIT)


def _swiglu_residual(x, g_ref, wg_ref, wu_ref, wd_ref):
    h = _rms(x, g_ref[...]).astype(BF16)
    acc = jnp.zeros_like(x)
    for c in range(0, D_FF, FF_CHUNK):
        w = min(FF_CHUNK, D_FF - c)
        g = jnp.dot(h, wg_ref[:, c:c + w].astype(BF16), preferred_element_type=F32)
        u = jnp.dot(h, wu_ref[:, c:c + w].astype(BF16), preferred_element_type=F32)
        a = (g * _sigmoid(g) * u).astype(BF16)
        acc = acc + jnp.dot(a, wd_ref[c:c + w, :].astype(BF16), preferred_element_type=F32)
    return x + 0.5 * acc


def _ffn_kernel(x_ref, g_ref, wg_ref, wu_ref, wd_ref, o_ref):
    o_ref[...] = _swiglu_residual(x_ref[...], g_ref, wg_ref, wu_ref, wd_ref)


def _mix_ffn_kernel(final, x_ref, oa_ref, ob_ref, oc_ref, wo_ref, g_ref, wg_ref, wu_ref, wd_ref,
                    fg_ref, o_ref):
    W = ATT_W
    wo = lambda lo, hi: wo_ref[lo:hi, :].astype(BF16)
    x = x_ref[...] + jnp.dot(oa_ref[...], wo(0, W), preferred_element_type=F32)
    x = x + jnp.dot(ob_ref[...], wo(W, 2 * W), preferred_element_type=F32)
    x = x + jnp.dot(oc_ref[...], wo(2 * W, D_MODEL), preferred_element_type=F32)
    y = _swiglu_residual(x, g_ref, wg_ref, wu_ref, wd_ref)
    if final:
        y = _rms(y, fg_ref[...])
    o_ref[...] = y


def _layer_spec(arr, layer):
    return pl.BlockSpec((None,) + arr.shape[1:], lambda i: (layer, 0, 0),
                        pipeline_mode=pl.Buffered(1))


def _ffn(x2d, layer, g, wg, wu, wd):
    n = x2d.shape[0]
    row = lambda i: (i, 0)
    return pl.pallas_call(
        _ffn_kernel,
        grid=(n // TM_FFN,),
        in_specs=[pl.BlockSpec((TM_FFN, D_MODEL), row), _layer_spec(g, layer),
                  _layer_spec(wg, layer), _layer_spec(wu, layer), _layer_spec(wd, layer)],
        out_specs=pl.BlockSpec((TM_FFN, D_MODEL), row),
        out_shape=jax.ShapeDtypeStruct((n, D_MODEL), F32),
        compiler_params=_params(("parallel",)),
        name="ffn",
    )(x2d, g, wg, wu, wd)


def _mix_ffn(x2d, oa, ob, oc, layer, wo, g, wg, wu, wd, fg, final):
    n = x2d.shape[0]
    row = lambda i: (i, 0)
    return pl.pallas_call(
        functools.partial(_mix_ffn_kernel, final),
        grid=(n // TM_FFN,),
        in_specs=[pl.BlockSpec((TM_FFN, D_MODEL), row),
                  pl.BlockSpec((TM_FFN, ATT_W), row),
                  pl.BlockSpec((TM_FFN, ATT_W), row),
                  pl.BlockSpec((TM_FFN, HG_W), row),
                  _layer_spec(wo, layer), _layer_spec(g, layer),
                  _layer_spec(wg, layer), _layer_spec(wu, layer), _layer_spec(wd, layer),
                  pl.BlockSpec((1, D_MODEL), lambda i: (0, 0))],
        out_specs=pl.BlockSpec((TM_FFN, D_MODEL), row),
        out_shape=jax.ShapeDtypeStruct((n, D_MODEL), F32),
        compiler_params=_params(("parallel",)),
        name="mix_ffn_final" if final else "mix_ffn",
    )(x2d, oa, ob, oc, wo, g, wg, wu, wd, fg)


def _inproj_kernel(x_ref, g_ref, wa_ref, wb_ref, cos_ref, sin_ref, fb_ref, lb_ref,
                   fk_ref, lf_ref, dk_ref, hq_ref, hl_ref, hk_ref, hv_ref, hs_ref,
                   fqt_ref, fvt_ref, dqt_ref, dvt_ref):
    h = _rms(x_ref[0], g_ref[...]).astype(BF16)
    W = ATT_W
    pa = jnp.dot(h, wa_ref[:, 0:3 * W + FF_LANES], preferred_element_type=F32)
    fk_ref[0] = pa[:, 0:W].astype(BF16)
    dk_ref[0] = (pa[:, W:2 * W] * cos_ref[0] + pa[:, 2 * W:3 * W] * sin_ref[0]).astype(BF16)
    z = pa[:, 3 * W:3 * W + FF_LANES] + fb_ref[...]
    lf_ref[0] = (jnp.minimum(z, 0.0) - jnp.log1p(jnp.exp(-jnp.abs(z)))) * LOG2E

    base = 3 * W + FF_LANES
    hdot = lambda i: jnp.dot(h, wa_ref[:, base + i * HG_W:base + (i + 1) * HG_W],
                             preferred_element_type=F32)
    q = hdot(0)
    hq_ref[0] = q * _sigmoid(q)
    z = hdot(1)
    lbf = jnp.clip(lb_ref[...], 0.0, 1.0 - 1e-6)
    ez = jnp.exp(-jnp.abs(z))
    inv = 1.0 / (1.0 + ez)
    sig_pos = jnp.where(z >= 0, inv, ez * inv)
    sig_neg = jnp.where(z >= 0, ez * inv, inv)
    hl_ref[0] = jnp.log(jnp.maximum(lbf, LB_FLOOR) + (1.0 - lbf) * sig_pos) * LOG2E
    hk_ref[0] = (1.0 - lbf) * sig_neg
    hv_ref[0] = hdot(2)
    hs_ref[0] = _sigmoid(hdot(3))

    pb = jnp.dot(h, wb_ref[...], preferred_element_type=F32)
    fqt_ref[0] = (pb[:, 0:W] * QK_SCALE).T.astype(BF16)
    dq = (pb[:, 2 * W:3 * W] * cos_ref[0] + pb[:, 3 * W:4 * W] * sin_ref[0]) * QK_SCALE
    dqt_ref[0] = dq.T.astype(BF16)
    ones = jnp.ones((VT_ROWS - HEAD_DIM, BK), BF16)
    for src, ref in ((pb[:, W:2 * W].T.astype(BF16), fvt_ref),
                     (pb[:, 4 * W:5 * W].T.astype(BF16), dvt_ref)):
        for j in range(TM_PROJ // BK):
            ref[0, j] = jnp.concatenate(
                [part for h in range(N_HEADS)
                 for part in (src[h * HEAD_DIM:(h + 1) * HEAD_DIM, j * BK:(j + 1) * BK], ones)],
                axis=0)


def _inproj(x, layer, g, wa, wb, cos, sin, fb, lb):
    B, T, _ = x.shape
    W = ATT_W
    nt = T // TM_PROJ
    per_layer = lambda arr: pl.BlockSpec((None,) + arr.shape[1:], lambda b, t: (layer, 0, 0),
                                         pipeline_mode=pl.Buffered(1))
    row = lambda b, t: (b, t, 0)
    col = lambda b, t: (b, 0, t)
    blk4 = lambda b, t: (b, t, 0, 0)
    hg_shape = jax.ShapeDtypeStruct((B, T, HG_W), F32)
    hg_spec = pl.BlockSpec((1, TM_PROJ, HG_W), row)
    out_shape = [
        jax.ShapeDtypeStruct((B, T, W), BF16),
        jax.ShapeDtypeStruct((B, T, FF_LANES), F32),
        jax.ShapeDtypeStruct((B, T, W), BF16),
        hg_shape, hg_shape, hg_shape, hg_shape, hg_shape,
        jax.ShapeDtypeStruct((B, W, T), BF16),
        jax.ShapeDtypeStruct((B, T // BK, N_HEADS * VT_ROWS, BK), BF16),
        jax.ShapeDtypeStruct((B, W, T), BF16),
        jax.ShapeDtypeStruct((B, T // BK, N_HEADS * VT_ROWS, BK), BF16),
    ]
    out_specs = [
        pl.BlockSpec((1, TM_PROJ, W), row),
        pl.BlockSpec((1, TM_PROJ, FF_LANES), row),
        pl.BlockSpec((1, TM_PROJ, W), row),
        hg_spec, hg_spec, hg_spec, hg_spec, hg_spec,
        pl.BlockSpec((1, W, TM_PROJ), col),
        pl.BlockSpec((1, TM_PROJ // BK, N_HEADS * VT_ROWS, BK), blk4),
        pl.BlockSpec((1, W, TM_PROJ), col),
        pl.BlockSpec((1, TM_PROJ // BK, N_HEADS * VT_ROWS, BK), blk4),
    ]
    return pl.pallas_call(
        _inproj_kernel,
        grid=(B, nt),
        in_specs=[
            pl.BlockSpec((1, TM_PROJ, D_MODEL), row),
            per_layer(g), per_layer(wa), per_layer(wb),
            pl.BlockSpec((1, TM_PROJ, W), row),
            pl.BlockSpec((1, TM_PROJ, W), row),
            per_layer(fb), per_layer(lb),
        ],
        out_specs=out_specs,
        out_shape=out_shape,
        compiler_params=_params(("parallel", "parallel")),
        name="inproj",
    )(x, g, wa, wb, cos, sin, fb, lb)


def _attn_kernel(fox, qt_ref, k_ref, vt_ref, aux_ref, *rest):
    if fox:
        sel_ref, o_ref, qm_ref, acc_ref, m_ref, st0_ref, st1_ref, ka_ref = rest
    else:
        o_ref, qm_ref, acc_ref, m_ref, st0_ref, st1_ref = rest
    qi = pl.program_id(1)
    T = k_ref.shape[1]
    R = BQ // BK
    assert R == 2
    qt = qt_ref[0]

    if fox:
        @pl.when(qi == 0)
        def _():
            tril3 = _tril3(CS_BLK)
            blocks = [slice(i * CS_BLK, (i + 1) * CS_BLK) for i in range(T // CS_BLK)]
            local = [_cumsum_rows(tril3, aux_ref[0, rows, :]) for rows in blocks]
            carry = jnp.zeros((1, FF_LANES), F32)
            for rows, loc in zip(blocks, local):
                blk = loc + carry
                carry = blk[CS_BLK - 1:CS_BLK, :]
                cat = jnp.concatenate([k_ref[0, rows, :], *_split3(blk)], axis=1)
                ka_ref[rows, :] = jnp.dot(cat, sel_ref[...],
                                          preferred_element_type=F32).astype(BF16)

        pad_row = lax.broadcasted_iota(jnp.int32, (HEAD_DIM, BQ), 0)
        minus_ones = jnp.where(pad_row < N_SPLIT, -1.0, 0.0).astype(BF16)
        for h in range(N_HEADS):
            qm_ref[h] = jnp.concatenate([qt[h * HEAD_DIM:(h + 1) * HEAD_DIM], minus_ones], axis=0)
    else:
        head_of_row = lax.broadcasted_iota(jnp.int32, qt.shape, 0) // HEAD_DIM
        for h in range(N_HEADS):
            qm_ref[h] = jnp.where(head_of_row == h, qt, jnp.zeros_like(qt))
    acc_ref[...] = jnp.zeros_like(acc_ref)
    krow = lax.broadcasted_iota(jnp.int32, (BK, BQ), 0)
    qcol = lax.broadcasted_iota(jnp.int32, (BK, BQ), 1)

    tri = (lax.broadcasted_iota(jnp.int32, (BK, BK), 0)
           <= lax.broadcasted_iota(jnp.int32, (BK, BK), 1))

    st_slots = (st0_ref, st1_ref)
    upper = slice(BK, BQ)

    def fetch(j, last=False):
        start = pl.multiple_of(j * BK, BK)
        if fox:
            kbs = [ka_ref[pl.ds(start, BK), 2 * h * HEAD_DIM:2 * (h + 1) * HEAD_DIM]
                   for h in range(N_HEADS)]
            aux = None
        else:
            kbs = [k_ref[0, pl.ds(start, BK), :]] * N_HEADS
            halves = (1,) if last else range(R)
            aux = jnp.concatenate([aux_ref[R * qi + half - j] for half in halves], axis=1)
        vbs = [vt_ref[0, j, h * VT_ROWS:(h + 1) * VT_ROWS, :] for h in range(N_HEADS)]
        return kbs, vbs, aux

    def stage(blk, slot, nxt=None, nxt_last=False, diag=False, last=False):
        _, vbs, aux = blk
        cols = upper if last else slice(0, BQ)
        nxt_cols = upper if nxt_last else slice(0, BQ)
        for h in range(N_HEADS):
            if nxt is not None:
                st_slots[1 - slot][h, :, nxt_cols] = jnp.dot(
                    nxt[0][h], qm_ref[h, :, nxt_cols], preferred_element_type=F32)
            st = st_slots[slot][h, :, cols]
            if not fox:
                st = st + aux
            elif last:
                st = jnp.where(tri, st, NEG_BIG)
            elif diag:
                st = jnp.where(krow <= qcol, st, NEG_BIG)
            m_old = m_ref[h, :, cols]
            m_new = jnp.maximum(m_old, jnp.max(st, axis=0, keepdims=True))
            m_ref[h, :, cols] = m_new
            p = jnp.exp2(st - m_new)
            alpha = jnp.exp2(m_old - m_new)
            rows = slice(h * VT_ROWS, (h + 1) * VT_ROWS)
            acc_ref[rows, cols] = alpha * acc_ref[rows, cols] + jnp.dot(
                vbs[h], p.astype(BF16), preferred_element_type=F32)

    def pair(i, _):
        b0, b1, b2 = fetch(2 * i), fetch(2 * i + 1), fetch(2 * i + 2)
        stage(b0, 0, nxt=b1)
        stage(b1, 1, nxt=b2)
        return 0

    m_ref[...] = jnp.full(m_ref.shape, NEG_BIG, F32)
    first = fetch(0)
    for h in range(N_HEADS):
        st_slots[0][h] = jnp.dot(first[0][h], qm_ref[h], preferred_element_type=F32)
    lax.fori_loop(0, qi, pair, 0)
    b0, b1 = fetch(R * qi), fetch(R * qi + 1, last=True)
    stage(b0, 0, nxt=b1, nxt_last=True, diag=True)
    stage(b1, 1, last=True)
    outs = []
    for h in range(N_HEADS):
        num = acc_ref[h * VT_ROWS:h * VT_ROWS + HEAD_DIM, :]
        den = acc_ref[h * VT_ROWS + HEAD_DIM:h * VT_ROWS + HEAD_DIM + 1, :]
        outs.append(num / den)
    o_ref[0] = jnp.concatenate(outs, axis=0).T.astype(BF16)


def _fox_slot_matrix():
    sel = np.zeros((ATT_W + N_SPLIT * FF_LANES, N_HEADS * 2 * HEAD_DIM), np.float32)
    for h in range(N_HEADS):
        for d in range(HEAD_DIM):
            sel[h * HEAD_DIM + d, 2 * h * HEAD_DIM + d] = 1.0
        for piece in range(N_SPLIT):
            sel[ATT_W + piece * FF_LANES + h, (2 * h + 1) * HEAD_DIM + piece] = 1.0
    return jnp.asarray(sel, BF16)


def _attention(fox, qt, k, vt, aux):
    B, T, W = k.shape
    nq = T // BQ
    in_specs = [
        pl.BlockSpec((1, W, BQ), lambda b, q: (b, 0, q)),
        pl.BlockSpec((1, T, W), lambda b, q: (b, 0, 0)),
        pl.BlockSpec((1, T // BK, N_HEADS * VT_ROWS, BK), lambda b, q: (b, 0, 0, 0)),
    ]
    scratch = [pltpu.VMEM((N_HEADS, 2 * HEAD_DIM if fox else W, BQ), BF16),
               pltpu.VMEM((N_HEADS * VT_ROWS, BQ), F32),
               pltpu.VMEM((N_HEADS, 1, BQ), F32),
               pltpu.VMEM((N_HEADS, BK, BQ), F32), pltpu.VMEM((N_HEADS, BK, BQ), F32)]
    if fox:
        sel = _fox_slot_matrix()
        operands = (qt, k, vt, aux, sel)
        in_specs += [pl.BlockSpec((1, T, FF_LANES), lambda b, q: (b, 0, 0)),
                     pl.BlockSpec(sel.shape, lambda b, q: (0, 0))]
        scratch.append(pltpu.VMEM((T, N_HEADS * 2 * HEAD_DIM), BF16))
    else:
        operands = (qt, k, vt, aux)
        in_specs.append(pl.BlockSpec(aux.shape, lambda b, q: (0, 0, 0)))
    return pl.pallas_call(
        functools.partial(_attn_kernel, fox),
        grid=(B, nq),
        in_specs=in_specs,
        out_specs=pl.BlockSpec((1, BQ, W), lambda b, q: (b, q, 0)),
        out_shape=jax.ShapeDtypeStruct((B, T, W), BF16),
        scratch_shapes=scratch,
        compiler_params=_params(("parallel", "arbitrary")),
        name="fox_attention" if fox else "dilated_attention",
    )(*operands)


def _dilated_bias_table():
    nd = max(w for w, _ in DILATED_PATTERNS) // BK
    d = (np.arange(nd)[:, None, None] * BK + np.arange(BK)[None, None, :]
         - np.arange(BK)[None, :, None])
    mult = np.zeros(d.shape, np.float64)
    for window, dil in DILATED_PATTERNS:
        mult += (d >= 0) & (d <= window) & (d % dil == 0)
    tab = np.where(mult > 0, np.log2(np.maximum(mult, 1.0)), NEG_BIG)
    return jnp.asarray(tab, F32)


def _hgrn_kernel(q_ref, lf_ref, k_ref, v_ref, s_ref, nw_ref, o_ref,
                 st_ref, b_scr, k_scr, a_scr, g_scr, bn_scr, ta_scr, ti_scr, tv_scr, tg_scr):
    C, S, E = HG_CHUNK, HG_SUB, HG_E
    NS = C // S
    n_chunks = HG_TB // C
    heads = range(HG_HEADS)
    lanes = [slice(hd * E, (hd + 1) * E) for hd in heads]

    @pl.when(pl.program_id(1) == 0)
    def _():
        st_ref[...] = jnp.zeros_like(st_ref)

    r = lax.broadcasted_iota(jnp.int32, (C, C), 0)
    c = lax.broadcasted_iota(jnp.int32, (C, C), 1)
    tril3 = _tril3(C)
    diag_mask = ((c // S) == (r // S)) & (c <= r)
    rr = lax.broadcasted_iota(jnp.int32, (S * E, C), 0)
    rc = lax.broadcasted_iota(jnp.int32, (S * E, C), 1)
    red = jnp.where(rr // E == rc % S, 1.0, 0.0).astype(BF16)
    first_row = lax.broadcasted_iota(jnp.int32, (NS, E), 0) == 0
    zeros_sub = jnp.zeros((S, E), F32)

    def rows_bcast(ref, hd, idx):
        return jnp.concatenate(
            [jnp.broadcast_to(ref[hd, idx(i):idx(i) + 1, :], (S, E)) for i in range(NS)], axis=0)

    def cumsum(ci):
        r0 = pl.multiple_of(ci * C, C)
        return [_cumsum_rows(tril3, lf_ref[0, pl.ds(r0, C), lanes[hd]]) for hd in heads]

    def out_dots():
        return [ti_scr[hd] + jnp.dot(ta_scr[hd], tv_scr[hd], preferred_element_type=F32)
                for hd in heads]

    def out_finish(ci, os_, gates, hd):
        r0 = pl.multiple_of(ci * C, C)
        o = os_[hd]
        o = o * lax.rsqrt(jnp.mean(o * o, axis=-1, keepdims=True) + EPS)
        o_ref[0, pl.ds(r0, C), lanes[hd]] = (o * gates[hd]).astype(BF16)

    def scores(ci, bs_, after_head):
        r0 = pl.multiple_of(ci * C, C)
        for hd in heads:
            b = bs_[hd]
            q = q_ref[0, pl.ds(r0, C), lanes[hd]]
            k = k_ref[0, pl.ds(r0, C), lanes[hd]]
            v = v_ref[0, pl.ds(r0, C), lanes[hd]]
            b_last = b[C - 1:C, :]
            st = st_ref[hd]
            qe = (q * jnp.exp2(b)).astype(BF16)
            ti_scr[hd] = lax.dot_general(qe, st.astype(BF16), (((1,), (1,)), ((), ())),
                                         preferred_element_type=F32)
            kdec = (k * jnp.exp2(b_last - b)).astype(BF16)
            st_ref[hd] = st * jnp.exp2(b_last) + jnp.dot(
                v.T.astype(BF16), kdec, preferred_element_type=F32)
            tv_scr[hd] = v.astype(BF16)
            tg_scr[hd] = s_ref[0, pl.ds(r0, C), lanes[hd]] * nw_ref[:, lanes[hd]]
            b_scr[hd] = b
            k_scr[hd] = k
            b_end = b_scr[hd, pl.ds(S - 1, NS, stride=S), :]
            b_prev = jnp.where(first_row, 0.0, pltpu.roll(b_end, 1, axis=0))
            a_scr[hd, 0:NS] = b_end
            a_scr[hd, NS:2 * NS] = b_prev
            q_t = q * jnp.exp2(b - rows_bcast(a_scr, hd, lambda i: NS + i))
            k_h = k * jnp.exp2(rows_bcast(a_scr, hd, lambda i: i) - b)
            for j in range(NS - 1):
                g_scr[hd, j] = jnp.exp2(jnp.minimum(b_prev - b_end[j:j + 1, :], 0.0))
            lhs_rows, rhs_rows = [], []
            for i in range(NS):
                qi_t = q_t[i * S:(i + 1) * S, :]
                lhs_rows.append(jnp.concatenate(
                    [qi_t * g_scr[hd, j, i:i + 1, :] if i > j else zeros_sub
                     for j in range(NS - 1)], axis=1))
                rhs_rows.append(jnp.concatenate(
                    [k_h[i * S:(i + 1) * S, :] if i == j else zeros_sub
                     for j in range(NS - 1)], axis=1))
            a_off = lax.dot_general(jnp.concatenate(lhs_rows, axis=0).astype(BF16),
                                    jnp.concatenate(rhs_rows, axis=0).astype(BF16),
                                    (((1,), (1,)), ((), ())),
                                    preferred_element_type=F32)
            w_parts = []
            for s in range(S):
                bs = rows_bcast(b_scr, hd, lambda i: i * S + s)
                ks = rows_bcast(k_scr, hd, lambda i: i * S + s)
                w_parts.append((jnp.exp2(jnp.minimum(b - bs, 0.0)) * (q * ks)).astype(BF16))
            a_diag = jnp.dot(jnp.concatenate(w_parts, axis=1), red,
                             preferred_element_type=F32)
            ta_scr[hd] = (a_off + jnp.where(diag_mask, a_diag, 0.0)).astype(BF16)
            after_head(hd)

    def step(ci, _):
        bs_ = [bn_scr[hd] for hd in heads]
        gates = [tg_scr[hd] for hd in heads]
        nxt = cumsum(jnp.minimum(ci + 1, n_chunks - 1))
        os_ = out_dots()
        scores(ci, bs_, functools.partial(out_finish, jnp.maximum(ci - 1, 0), os_, gates))
        for hd in heads:
            bn_scr[hd] = nxt[hd]
        return 0

    for ref in (ta_scr, ti_scr, tv_scr, tg_scr):
        ref[...] = jnp.zeros_like(ref)
    first = cumsum(0)
    for hd in heads:
        bn_scr[hd] = first[hd]
    lax.fori_loop(0, n_chunks, step, 0)
    last_os, last_gates = out_dots(), [tg_scr[hd] for hd in heads]
    for hd in heads:
        out_finish(n_chunks - 1, last_os, last_gates, hd)


def _hgrn(hq, hl, hk, hv, hs, layer, nw):
    B, T, W = hq.shape
    row = lambda b, t: (b, t, 0)
    blk = pl.BlockSpec((1, HG_TB, W), row)
    C, NS = HG_CHUNK, HG_CHUNK // HG_SUB
    per_head = lambda *shape, dtype=F32: pltpu.VMEM((HG_HEADS,) + shape, dtype)
    return pl.pallas_call(
        _hgrn_kernel,
        grid=(B, T // HG_TB),
        in_specs=[blk, blk, blk, blk, blk,
                  pl.BlockSpec((None, 1, W), lambda b, t: (layer, 0, 0))],
        out_specs=pl.BlockSpec((1, HG_TB, W), row),
        out_shape=jax.ShapeDtypeStruct((B, T, W), BF16),
        scratch_shapes=[
            per_head(HG_V, HG_E),
            per_head(C, HG_E),
            per_head(C, HG_E),
            per_head(2 * NS, HG_E),
            per_head(NS - 1, NS, HG_E),
            per_head(C, HG_E),
            per_head(C, C, dtype=BF16),
            per_head(C, HG_V),
            per_head(C, HG_V, dtype=BF16),
            per_head(C, HG_V),
        ],
        compiler_params=_params(("parallel", "arbitrary")),
        name="hgrn2",
    )(hq, hl, hk, hv, hs, nw)


def _rot_cols(w):
    half = ROPE_DIM // 2
    w4 = w.reshape(w.shape[:-1] + (N_HEADS, HEAD_DIM))
    rot = jnp.concatenate([-w4[..., half:ROPE_DIM], w4[..., :half],
                           jnp.zeros_like(w4[..., ROPE_DIM:])], axis=-1)
    return rot.reshape(w.shape)


def _rope_tables(positions):
    freqs = ROPE_THETA ** (-jnp.arange(0, ROPE_DIM, 2, dtype=F32) / ROPE_DIM)
    ang = positions.astype(F32)[:, :, None] * freqs
    B, T = positions.shape
    ones = jnp.ones((B, T, HEAD_DIM - ROPE_DIM), F32)
    cos_h = jnp.concatenate([jnp.cos(ang), jnp.cos(ang), ones], axis=-1)
    sin_h = jnp.concatenate([jnp.sin(ang), jnp.sin(ang), 0.0 * ones], axis=-1)
    return jnp.tile(cos_h, (1, 1, N_HEADS)), jnp.tile(sin_h, (1, 1, N_HEADS))


def _split_w_in(w):
    W = ATT_W
    sizes = (W, W, W, N_HEADS, W, W, W, HG_W, HG_W, HG_W, HG_W)
    pts = tuple(int(s) for s in np.cumsum(sizes)[:-1])
    return jnp.split(w, pts, axis=-1)


def kernel(x, positions, ffn1_norm, ffn1_w_gate, ffn1_w_up, ffn1_w_down, mix_norm, w_in,
           fox_forget_bias, hgrn_lower_bounds, hgrn_out_norm, w_out, ffn2_norm,
           ffn2_w_gate, ffn2_w_up, ffn2_w_down, final_norm):
    B, T, D = x.shape
    depth = w_in.shape[0]
    assert D == D_MODEL and T % max(TM_PROJ, HG_TB, BQ) == 0 and (B * T) % TM_FFN == 0
    assert BQ % BK == 0 and TM_PROJ % BK == 0

    sm = jax.nn.softmax(hgrn_lower_bounds.astype(F32), axis=0)
    lbs = (jnp.cumsum(sm, axis=0) - sm[0:1]).reshape(depth, 1, HG_W)
    fq, fk, fv, ffw, dq, dk, dv, hq, hf, hi, hg = _split_w_in(w_in.astype(BF16))
    ffw_pad = jnp.pad(ffw, ((0, 0), (0, 0), (0, FF_LANES - N_HEADS)))
    wa = jnp.concatenate([fk, dk, _rot_cols(dk), ffw_pad, hq, hf, hi, hg], axis=-1)
    wb = jnp.concatenate([fq, fv, dq, _rot_cols(dq), dv], axis=-1)
    fb = jnp.pad(fox_forget_bias.astype(F32), ((0, 0), (0, FF_LANES - N_HEADS)))
    fb = fb.reshape(depth, 1, FF_LANES)
    stack_row = lambda p: p.astype(F32).reshape(depth, 1, p.shape[-1])
    n1, nm, n2, nh = (stack_row(p) for p in (ffn1_norm, mix_norm, ffn2_norm, hgrn_out_norm))
    w1 = (ffn1_w_gate, ffn1_w_up, ffn1_w_down)
    w2 = (ffn2_w_gate, ffn2_w_up, ffn2_w_down)
    wo = w_out
    fnorm = final_norm.astype(F32).reshape(1, D)

    cos, sin = _rope_tables(positions)
    dil_tab = _dilated_bias_table()

    xf = x.reshape(B * T, D)
    for i in range(depth):
        xf = _ffn(xf, i, n1, *w1)
        (fk_a, lf_a, dk_a, hq_a, hl_a, hk_a, hv_a, hs_a, fqt_a, fvt_a, dqt_a, dvt_a) = _inproj(
            xf.reshape(B, T, D), i, nm, wa, wb, cos, sin, fb, lbs)
        oa = _attention(True, fqt_a, fk_a, fvt_a, lf_a)
        ob = _attention(False, dqt_a, dk_a, dvt_a, dil_tab)
        oc = _hgrn(hq_a, hl_a, hk_a, hv_a, hs_a, i, nh)
        xf = _mix_ffn(xf, oa.reshape(B * T, ATT_W), ob.reshape(B * T, ATT_W),
                      oc.reshape(B * T, HG_W), i, wo, n2, *w2, fnorm, i == depth - 1)
    return xf.reshape(B, T, D)
```

```python
import functools
import math

import numpy as np
import jax
import jax.numpy as jnp
from jax import lax
from jax.experimental import pallas as pl
from jax.experimental.pallas import tpu as pltpu

F32 = jnp.float32
BF16 = jnp.bfloat16

D_MODEL = 1024
HEAD_DIM = 64
N_HEADS = 4
ATT_W = N_HEADS * HEAD_DIM
HG_HEADS = 4
HG_E = 128
HG_V = 128
HG_W = HG_HEADS * HG_V
ROPE_DIM = HEAD_DIM // 4
ROPE_THETA = 500000.0
D_FF = ((8 * D_MODEL // 3 + 127) // 128) * 128
EPS = 1e-6
NEG_BIG = -1e30
LB_FLOOR = 1e-30
DILATED_PATTERNS = ((128, 1), (512, 4), (2048, 16))
FF_LANES = 128
VT_ROWS = HEAD_DIM + 16
LOG2E = 1.0 / math.log(2.0)
QK_SCALE = HEAD_DIM ** -0.5 * LOG2E
N_SPLIT = 3

V7X_VMEM_BYTES = 64 * 1024 * 1024
VMEM_LIMIT = V7X_VMEM_BYTES - 8 * 1024 * 1024
SUBLANES = 8
LANES = 128

TM_FFN = 512
FF_CHUNK = 512
TM_PROJ = 512
BQ = 512
BK = 256
HG_CHUNK = 64
HG_SUB = SUBLANES
HG_TB = 1024
CS_BLK = 256


def _rms(x, g):
    ms = jnp.mean(x * x, axis=-1, keepdims=True)
    return x * lax.rsqrt(ms + EPS) * g


def _sigmoid(x):
    return 1.0 / (1.0 + jnp.exp(-x))


def _split3(x):
    hi = x.astype(BF16)
    rest = x - hi.astype(F32)
    mid = rest.astype(BF16)
    lo = (rest - mid.astype(F32)).astype(BF16)
    return hi, mid, lo


def _tril3(n):
    width = -(-N_SPLIT * n // LANES) * LANES
    r = lax.broadcasted_iota(jnp.int32, (n, width), 0)
    c = lax.broadcasted_iota(jnp.int32, (n, width), 1)
    return jnp.where((c % n <= r) & (c < N_SPLIT * n), 1.0, 0.0).astype(BF16)


def _cumsum_rows(tril3, x):
    pieces = list(_split3(x))
    pad = tril3.shape[1] - N_SPLIT * x.shape[0]
    if pad:
        pieces.append(jnp.zeros((pad, x.shape[1]), BF16))
    return jnp.dot(tril3, jnp.concatenate(pieces, axis=0), preferred_element_type=F32)


def _params(sem):
    return pltpu.CompilerParams(dimension_semantics=sem, vmem_limit_bytes=VMEM_LIMIT)


def _swiglu_residual(x, g_ref, wg_ref, wu_ref, wd_ref):
    h = _rms(x, g_ref[...]).astype(BF16)
    acc = jnp.zeros_like(x)
    for c in range(0, D_FF, FF_CHUNK):
        w = min(FF_CHUNK, D_FF - c)
        g = jnp.dot(h, wg_ref[:, c:c + w].astype(BF16), preferred_element_type=F32)
        u = jnp.dot(h, wu_ref[:, c:c + w].astype(BF16), preferred_element_type=F32)
        a = (g * _sigmoid(g) * u).astype(BF16)
        acc = acc + jnp.dot(a, wd_ref[c:c + w, :].astype(BF16), preferred_element_type=F32)
    return x + 0.5 * acc


def _ffn_kernel(x_ref, g_ref, wg_ref, wu_ref, wd_ref, o_ref):
    o_ref[...] = _swiglu_residual(x_ref[...], g_ref, wg_ref, wu_ref, wd_ref)


def _mix_ffn_kernel(final, x_ref, oa_ref, ob_ref, oc_ref, wo_ref, g_ref, wg_ref, wu_ref, wd_ref,
                    fg_ref, o_ref):
    W = ATT_W
    wo = lambda lo, hi: wo_ref[lo:hi, :].astype(BF16)
    x = x_ref[...] + jnp.dot(oa_ref[...], wo(0, W), preferred_element_type=F32)
    x = x + jnp.dot(ob_ref[...], wo(W, 2 * W), preferred_element_type=F32)
    x = x + jnp.dot(oc_ref[...], wo(2 * W, D_MODEL), preferred_element_type=F32)
    y = _swiglu_residual(x, g_ref, wg_ref, wu_ref, wd_ref)
    if final:
        y = _rms(y, fg_ref[...])
    o_ref[...] = y


def _layer_spec(arr, layer):
    return pl.BlockSpec((None,) + arr.shape[1:], lambda i: (layer, 0, 0),
                        pipeline_mode=pl.Buffered(1))


def _ffn(x2d, layer, g, wg, wu, wd):
    n = x2d.shape[0]
    row = lambda i: (i, 0)
    return pl.pallas_call(
        _ffn_kernel,
        grid=(n // TM_FFN,),
        in_specs=[pl.BlockSpec((TM_FFN, D_MODEL), row), _layer_spec(g, layer),
                  _layer_spec(wg, layer), _layer_spec(wu, layer), _layer_spec(wd, layer)],
        out_specs=pl.BlockSpec((TM_FFN, D_MODEL), row),
        out_shape=jax.ShapeDtypeStruct((n, D_MODEL), F32),
        compiler_params=_params(("parallel",)),
        name="ffn",
    )(x2d, g, wg, wu, wd)


def _mix_ffn(x2d, oa, ob, oc, layer, wo, g, wg, wu, wd, fg, final):
    n = x2d.shape[0]
    row = lambda i: (i, 0)
    return pl.pallas_call(
        functools.partial(_mix_ffn_kernel, final),
        grid=(n // TM_FFN,),
        in_specs=[pl.BlockSpec((TM_FFN, D_MODEL), row),
                  pl.BlockSpec((TM_FFN, ATT_W), row),
                  pl.BlockSpec((TM_FFN, ATT_W), row),
                  pl.BlockSpec((TM_FFN, HG_W), row),
                  _layer_spec(wo, layer), _layer_spec(g, layer),
                  _layer_spec(wg, layer), _layer_spec(wu, layer), _layer_spec(wd, layer),
                  pl.BlockSpec((1, D_MODEL), lambda i: (0, 0))],
        out_specs=pl.BlockSpec((TM_FFN, D_MODEL), row),
        out_shape=jax.ShapeDtypeStruct((n, D_MODEL), F32),
        compiler_params=_params(("parallel",)),
        name="mix_ffn_final" if final else "mix_ffn",
    )(x2d, oa, ob, oc, wo, g, wg, wu, wd, fg)


def _inproj_kernel(x_ref, g_ref, wa_ref, wb_ref, cos_ref, sin_ref, fb_ref, lb_ref,
                   fk_ref, lf_ref, dk_ref, hq_ref, hl_ref, hk_ref, hv_ref, hs_ref,
                   fqt_ref, fvt_ref, dqt_ref, dvt_ref):
    h = _rms(x_ref[0], g_ref[...]).astype(BF16)
    W = ATT_W
    pa = jnp.dot(h, wa_ref[:, 0:3 * W + FF_LANES], preferred_element_type=F32)
    fk_ref[0] = pa[:, 0:W].astype(BF16)
    dk_ref[0] = (pa[:, W:2 * W] * cos_ref[0] + pa[:, 2 * W:3 * W] * sin_ref[0]).astype(BF16)
    z = pa[:, 3 * W:3 * W + FF_LANES] + fb_ref[...]
    lf_ref[0] = (jnp.minimum(z, 0.0) - jnp.log1p(jnp.exp(-jnp.abs(z)))) * LOG2E

    base = 3 * W + FF_LANES
    hdot = lambda i: jnp.dot(h, wa_ref[:, base + i * HG_W:base + (i + 1) * HG_W],
                             preferred_element_type=F32)
    q = hdot(0)
    hq_ref[0] = q * _sigmoid(q)
    z = hdot(1)
    lbf = jnp.clip(lb_ref[...], 0.0, 1.0 - 1e-6)
    ez = jnp.exp(-jnp.abs(z))
    inv = 1.0 / (1.0 + ez)
    sig_pos = jnp.where(z >= 0, inv, ez * inv)
    sig_neg = jnp.where(z >= 0, ez * inv, inv)
    hl_ref[0] = jnp.log(jnp.maximum(lbf, LB_FLOOR) + (1.0 - lbf) * sig_pos) * LOG2E
    hk_ref[0] = (1.0 - lbf) * sig_neg
    hv_ref[0] = hdot(2)
    hs_ref[0] = _sigmoid(hdot(3))

    pb = jnp.dot(h, wb_ref[...], preferred_element_type=F32)
    fqt_ref[0] = (pb[:, 0:W] * QK_SCALE).T.astype(BF16)
    dq = (pb[:, 2 * W:3 * W] * cos_ref[0] + pb[:, 3 * W:4 * W] * sin_ref[0]) * QK_SCALE
    dqt_ref[0] = dq.T.astype(BF16)
    ones = jnp.ones((VT_ROWS - HEAD_DIM, BK), BF16)
    for src, ref in ((pb[:, W:2 * W].T.astype(BF16), fvt_ref),
                     (pb[:, 4 * W:5 * W].T.astype(BF16), dvt_ref)):
        for j in range(TM_PROJ // BK):
            ref[0, j] = jnp.concatenate(
                [part for h in range(N_HEADS)
                 for part in (src[h * HEAD_DIM:(h + 1) * HEAD_DIM, j * BK:(j + 1) * BK], ones)],
                axis=0)


def _inproj(x, layer, g, wa, wb, cos, sin, fb, lb):
    B, T, _ = x.shape
    W = ATT_W
    nt = T // TM_PROJ
    per_layer = lambda arr: pl.BlockSpec((None,) + arr.shape[1:], lambda b, t: (layer, 0, 0),
                                         pipeline_mode=pl.Buffered(1))
    row = lambda b, t: (b, t, 0)
    col = lambda b, t: (b, 0, t)
    blk4 = lambda b, t: (b, t, 0, 0)
    hg_shape = jax.ShapeDtypeStruct((B, T, HG_W), F32)
    hg_spec = pl.BlockSpec((1, TM_PROJ, HG_W), row)
    out_shape = [
        jax.ShapeDtypeStruct((B, T, W), BF16),
        jax.ShapeDtypeStruct((B, T, FF_LANES), F32),
        jax.ShapeDtypeStruct((B, T, W), BF16),
        hg_shape, hg_shape, hg_shape, hg_shape, hg_shape,
        jax.ShapeDtypeStruct((B, W, T), BF16),
        jax.ShapeDtypeStruct((B, T // BK, N_HEADS * VT_ROWS, BK), BF16),
        jax.ShapeDtypeStruct((B, W, T), BF16),
        jax.ShapeDtypeStruct((B, T // BK, N_HEADS * VT_ROWS, BK), BF16),
    ]
    out_specs = [
        pl.BlockSpec((1, TM_PROJ, W), row),
        pl.BlockSpec((1, TM_PROJ, FF_LANES), row),
        pl.BlockSpec((1, TM_PROJ, W), row),
        hg_spec, hg_spec, hg_spec, hg_spec, hg_spec,
        pl.BlockSpec((1, W, TM_PROJ), col),
        pl.BlockSpec((1, TM_PROJ // BK, N_HEADS * VT_ROWS, BK), blk4),
        pl.BlockSpec((1, W, TM_PROJ), col),
        pl.BlockSpec((1, TM_PROJ // BK, N_HEADS * VT_ROWS, BK), blk4),
    ]
    return pl.pallas_call(
        _inproj_kernel,
        grid=(B, nt),
        in_specs=[
            pl.BlockSpec((1, TM_PROJ, D_MODEL), row),
            per_layer(g), per_layer(wa), per_layer(wb),
            pl.BlockSpec((1, TM_PROJ, W), row),
            pl.BlockSpec((1, TM_PROJ, W), row),
            per_layer(fb), per_layer(lb),
        ],
        out_specs=out_specs,
        out_shape=out_shape,
        compiler_params=_params(("parallel", "parallel")),
        name="inproj",
    )(x, g, wa, wb, cos, sin, fb, lb)


def _attn_pipeline(fox, qi, qt_ref, k_ref, vt_ref, aux_ref, sel_ref, o_ref,
                   qm_ref, acc_ref, m_ref, st_slots, ka_ref):
    T = k_ref.shape[1]
    R = BQ // BK
    assert R == 2
    qt = qt_ref[0]

    if fox:
        @pl.when(qi == 0)
        def _():
            tril3 = _tril3(CS_BLK)
            blocks = [slice(i * CS_BLK, (i + 1) * CS_BLK) for i in range(T // CS_BLK)]
            local = [_cumsum_rows(tril3, aux_ref[0, rows, :]) for rows in blocks]
            carry = jnp.zeros((1, FF_LANES), F32)
            for rows, loc in zip(blocks, local):
                blk = loc + carry
                carry = blk[CS_BLK - 1:CS_BLK, :]
                cat = jnp.concatenate([k_ref[0, rows, :], *_split3(blk)], axis=1)
                ka_ref[rows, :] = jnp.dot(cat, sel_ref[...],
                                          preferred_element_type=F32).astype(BF16)

        pad_row = lax.broadcasted_iota(jnp.int32, (HEAD_DIM, BQ), 0)
        minus_ones = jnp.where(pad_row < N_SPLIT, -1.0, 0.0).astype(BF16)
        for h in range(N_HEADS):
            qm_ref[h] = jnp.concatenate([qt[h * HEAD_DIM:(h + 1) * HEAD_DIM], minus_ones], axis=0)
    else:
        head_of_row = lax.broadcasted_iota(jnp.int32, qt.shape, 0) // HEAD_DIM
        for h in range(N_HEADS):
            qm_ref[h] = jnp.where(head_of_row == h, qt, jnp.zeros_like(qt))
    acc_ref[...] = jnp.zeros_like(acc_ref)
    m_ref[...] = jnp.full(m_ref.shape, NEG_BIG, F32)
    krow = lax.broadcasted_iota(jnp.int32, (BK, BQ), 0)
    qcol = lax.broadcasted_iota(jnp.int32, (BK, BQ), 1)
    tri = (lax.broadcasted_iota(jnp.int32, (BK, BK), 0)
           <= lax.broadcasted_iota(jnp.int32, (BK, BK), 1))
    upper = slice(BK, BQ)

    def fetch(j, last=False):
        start = pl.multiple_of(j * BK, BK)
        if fox:
            kbs = [ka_ref[pl.ds(start, BK), 2 * h * HEAD_DIM:2 * (h + 1) * HEAD_DIM]
                   for h in range(N_HEADS)]
            aux = None
        else:
            kbs = [k_ref[0, pl.ds(start, BK), :]] * N_HEADS
            halves = (1,) if last else range(R)
            aux = jnp.concatenate([aux_ref[R * qi + half - j] for half in halves], axis=1)
        vbs = [vt_ref[0, j, h * VT_ROWS:(h + 1) * VT_ROWS, :] for h in range(N_HEADS)]
        return kbs, vbs, aux

    def head_stage(h, blk, slot, nxt=None, nxt_last=False, diag=False, last=False):
        _, vbs, aux = blk
        cols = upper if last else slice(0, BQ)
        nxt_cols = upper if nxt_last else slice(0, BQ)
        if nxt is not None:
            st_slots[1 - slot][h, :, nxt_cols] = jnp.dot(
                nxt[0][h], qm_ref[h, :, nxt_cols], preferred_element_type=F32)
        st = st_slots[slot][h, :, cols]
        if not fox:
            st = st + aux
        elif last:
            st = jnp.where(tri, st, NEG_BIG)
        elif diag:
            st = jnp.where(krow <= qcol, st, NEG_BIG)
        m_old = m_ref[h, :, cols]
        m_new = jnp.maximum(m_old, jnp.max(st, axis=0, keepdims=True))
        m_ref[h, :, cols] = m_new
        p = jnp.exp2(st - m_new)
        alpha = jnp.exp2(m_old - m_new)
        rows = slice(h * VT_ROWS, (h + 1) * VT_ROWS)
        acc_ref[rows, cols] = alpha * acc_ref[rows, cols] + jnp.dot(
            vbs[h], p.astype(BF16), preferred_element_type=F32)

    def first_scores(h, blk):
        st_slots[0][h] = jnp.dot(blk[0][h], qm_ref[h], preferred_element_type=F32)

    def finish():
        outs = []
        for h in range(N_HEADS):
            num = acc_ref[h * VT_ROWS:h * VT_ROWS + HEAD_DIM, :]
            den = acc_ref[h * VT_ROWS + HEAD_DIM:h * VT_ROWS + HEAD_DIM + 1, :]
            outs.append(num / den)
        o_ref[0] = jnp.concatenate(outs, axis=0).T.astype(BF16)

    return fetch, head_stage, first_scores, finish


def _attn_kernel(fqt_ref, fk_ref, fvt_ref, lf_ref, sel_ref, dqt_ref, dk_ref, dvt_ref, tab_ref,
                 of_ref, od_ref,
                 fqm, facc, fm, fst0, fst1, fka, dqm, dacc, dm, dst0, dst1):
    qi = pl.program_id(1)
    R = BQ // BK
    groups = (
        _attn_pipeline(True, qi, fqt_ref, fk_ref, fvt_ref, lf_ref, sel_ref, of_ref,
                       fqm, facc, fm, (fst0, fst1), fka),
        _attn_pipeline(False, qi, dqt_ref, dk_ref, dvt_ref, tab_ref, None, od_ref,
                       dqm, dacc, dm, (dst0, dst1), None),
    )

    def stage(blks, slot, nxts=None, **flags):
        for h in range(N_HEADS):
            for g, (_, head_stage, _, _) in enumerate(groups):
                head_stage(h, blks[g], slot, nxt=None if nxts is None else nxts[g], **flags)

    def fetch_all(j, **kw):
        return [fetch(j, **kw) for fetch, _, _, _ in groups]

    def pair(i, _):
        b0, b1, b2 = fetch_all(2 * i), fetch_all(2 * i + 1), fetch_all(2 * i + 2)
        stage(b0, 0, b1)
        stage(b1, 1, b2)
        return 0

    first = fetch_all(0)
    for h in range(N_HEADS):
        for g, (_, _, first_scores, _) in enumerate(groups):
            first_scores(h, first[g])
    lax.fori_loop(0, qi, pair, 0)
    b0, b1 = fetch_all(R * qi), fetch_all(R * qi + 1, last=True)
    stage(b0, 0, b1, nxt_last=True, diag=True)
    stage(b1, 1, last=True)
    for _, _, _, finish in groups:
        finish()


def _fox_slot_matrix():
    sel = np.zeros((ATT_W + N_SPLIT * FF_LANES, N_HEADS * 2 * HEAD_DIM), np.float32)
    for h in range(N_HEADS):
        for d in range(HEAD_DIM):
            sel[h * HEAD_DIM + d, 2 * h * HEAD_DIM + d] = 1.0
        for piece in range(N_SPLIT):
            sel[ATT_W + piece * FF_LANES + h, (2 * h + 1) * HEAD_DIM + piece] = 1.0
    return jnp.asarray(sel, BF16)


def _attention(fqt, fk, fvt, lf, dqt, dk, dvt, tab):
    B, T, W = fk.shape
    sel = _fox_slot_matrix()
    q_spec = pl.BlockSpec((1, W, BQ), lambda b, q: (b, 0, q))
    k_spec = pl.BlockSpec((1, T, W), lambda b, q: (b, 0, 0))
    v_spec = pl.BlockSpec((1, T // BK, N_HEADS * VT_ROWS, BK), lambda b, q: (b, 0, 0, 0))
    o_spec = pl.BlockSpec((1, BQ, W), lambda b, q: (b, q, 0))
    o_shape = jax.ShapeDtypeStruct((B, T, W), BF16)

    def group_scratch(q_rows):
        return [pltpu.VMEM((N_HEADS, q_rows, BQ), BF16),
                pltpu.VMEM((N_HEADS * VT_ROWS, BQ), F32),
                pltpu.VMEM((N_HEADS, 1, BQ), F32),
                pltpu.VMEM((N_HEADS, BK, BQ), F32), pltpu.VMEM((N_HEADS, BK, BQ), F32)]

    return pl.pallas_call(
        _attn_kernel,
        grid=(B, T // BQ),
        in_specs=[q_spec, k_spec, v_spec,
                  pl.BlockSpec((1, T, FF_LANES), lambda b, q: (b, 0, 0)),
                  pl.BlockSpec(sel.shape, lambda b, q: (0, 0)),
                  q_spec, k_spec, v_spec,
                  pl.BlockSpec(tab.shape, lambda b, q: (0, 0, 0))],
        out_specs=[o_spec, o_spec],
        out_shape=[o_shape, o_shape],
        scratch_shapes=(group_scratch(2 * HEAD_DIM)
                        + [pltpu.VMEM((T, N_HEADS * 2 * HEAD_DIM), BF16)]
                        + group_scratch(W)),
        compiler_params=_params(("parallel", "arbitrary")),
        name="attention",
    )(fqt, fk, fvt, lf, sel, dqt, dk, dvt, tab)


def _dilated_bias_table():
    nd = max(w for w, _ in DILATED_PATTERNS) // BK
    d = (np.arange(nd)[:, None, None] * BK + np.arange(BK)[None, None, :]
         - np.arange(BK)[None, :, None])
    mult = np.zeros(d.shape, np.float64)
    for window, dil in DILATED_PATTERNS:
        mult += (d >= 0) & (d <= window) & (d % dil == 0)
    tab = np.where(mult > 0, np.log2(np.maximum(mult, 1.0)), NEG_BIG)
    return jnp.asarray(tab, F32)


def _hgrn_kernel(q_ref, lf_ref, k_ref, v_ref, s_ref, nw_ref, o_ref,
                 st_ref, b_scr, k_scr, a_scr, g_scr, bn_scr, ta_scr, ti_scr, tv_scr, tg_scr):
    C, S, E = HG_CHUNK, HG_SUB, HG_E
    NS = C // S
    n_chunks = HG_TB // C
    heads = range(HG_HEADS)
    lanes = [slice(hd * E, (hd + 1) * E) for hd in heads]

    @pl.when(pl.program_id(1) == 0)
    def _():
        st_ref[...] = jnp.zeros_like(st_ref)

    r = lax.broadcasted_iota(jnp.int32, (C, C), 0)
    c = lax.broadcasted_iota(jnp.int32, (C, C), 1)
    tril3 = _tril3(C)
    diag_mask = ((c // S) == (r // S)) & (c <= r)
    rr = lax.broadcasted_iota(jnp.int32, (S * E, C), 0)
    rc = lax.broadcasted_iota(jnp.int32, (S * E, C), 1)
    red = jnp.where(rr // E == rc % S, 1.0, 0.0).astype(BF16)
    first_row = lax.broadcasted_iota(jnp.int32, (NS, E), 0) == 0
    zeros_sub = jnp.zeros((S, E), F32)

    def rows_bcast(ref, hd, idx):
        return jnp.concatenate(
            [jnp.broadcast_to(ref[hd, idx(i):idx(i) + 1, :], (S, E)) for i in range(NS)], axis=0)

    def cumsum(ci):
        r0 = pl.multiple_of(ci * C, C)
        return [_cumsum_rows(tril3, lf_ref[0, pl.ds(r0, C), lanes[hd]]) for hd in heads]

    def out_dots():
        return [ti_scr[hd] + jnp.dot(ta_scr[hd], tv_scr[hd], preferred_element_type=F32)
                for hd in heads]

    def out_finish(ci, os_, gates, hd):
        r0 = pl.multiple_of(ci * C, C)
        o = os_[hd]
        o = o * lax.rsqrt(jnp.mean(o * o, axis=-1, keepdims=True) + EPS)
        o_ref[0, pl.ds(r0, C), lanes[hd]] = (o * gates[hd]).astype(BF16)

    def scores(ci, bs_, after_head):
        r0 = pl.multiple_of(ci * C, C)
        for hd in heads:
            b = bs_[hd]
            q = q_ref[0, pl.ds(r0, C), lanes[hd]]
            k = k_ref[0, pl.ds(r0, C), lanes[hd]]
            v = v_ref[0, pl.ds(r0, C), lanes[hd]]
            b_last = b[C - 1:C, :]
            st = st_ref[hd]
            qe = (q * jnp.exp2(b)).astype(BF16)
            ti_scr[hd] = lax.dot_general(qe, st.astype(BF16), (((1,), (1,)), ((), ())),
                                         preferred_element_type=F32)
            kdec = (k * jnp.exp2(b_last - b)).astype(BF16)
            st_ref[hd] = st * jnp.exp2(b_last) + jnp.dot(
                v.T.astype(BF16), kdec, preferred_element_type=F32)
            tv_scr[hd] = v.astype(BF16)
            tg_scr[hd] = s_ref[0, pl.ds(r0, C), lanes[hd]] * nw_ref[:, lanes[hd]]
            b_scr[hd] = b
            k_scr[hd] = k
            b_end = b_scr[hd, pl.ds(S - 1, NS, stride=S), :]
            b_prev = jnp.where(first_row, 0.0, pltpu.roll(b_end, 1, axis=0))
            a_scr[hd, 0:NS] = b_end
            a_scr[hd, NS:2 * NS] = b_prev
            q_t = q * jnp.exp2(b - rows_bcast(a_scr, hd, lambda i: NS + i))
            k_h = k * jnp.exp2(rows_bcast(a_scr, hd, lambda i: i) - b)
            for j in range(NS - 1):
                g_scr[hd, j] = jnp.exp2(jnp.minimum(b_prev - b_end[j:j + 1, :], 0.0))
            lhs_rows, rhs_rows = [], []
            for i in range(NS):
                qi_t = q_t[i * S:(i + 1) * S, :]
                lhs_rows.append(jnp.concatenate(
                    [qi_t * g_scr[hd, j, i:i + 1, :] if i > j else zeros_sub
                     for j in range(NS - 1)], axis=1))
                rhs_rows.append(jnp.concatenate(
                    [k_h[i * S:(i + 1) * S, :] if i == j else zeros_sub
                     for j in range(NS - 1)], axis=1))
            a_off = lax.dot_general(jnp.concatenate(lhs_rows, axis=0).astype(BF16),
                                    jnp.concatenate(rhs_rows, axis=0).astype(BF16),
                                    (((1,), (1,)), ((), ())),
                                    preferred_element_type=F32)
            w_parts = []
            for s in range(S):
                bs = rows_bcast(b_scr, hd, lambda i: i * S + s)
                ks = rows_bcast(k_scr, hd, lambda i: i * S + s)
                w_parts.append((jnp.exp2(jnp.minimum(b - bs, 0.0)) * (q * ks)).astype(BF16))
            a_diag = jnp.dot(jnp.concatenate(w_parts, axis=1), red,
                             preferred_element_type=F32)
            ta_scr[hd] = (a_off + jnp.where(diag_mask, a_diag, 0.0)).astype(BF16)
            after_head(hd)

    def step(ci, _):
        bs_ = [bn_scr[hd] for hd in heads]
        gates = [tg_scr[hd] for hd in heads]
        nxt = cumsum(jnp.minimum(ci + 1, n_chunks - 1))
        os_ = out_dots()
        scores(ci, bs_, functools.partial(out_finish, jnp.maximum(ci - 1, 0), os_, gates))
        for hd in heads:
            bn_scr[hd] = nxt[hd]
        return 0

    for ref in (ta_scr, ti_scr, tv_scr, tg_scr):
        ref[...] = jnp.zeros_like(ref)
    first = cumsum(0)
    for hd in heads:
        bn_scr[hd] = first[hd]
    lax.fori_loop(0, n_chunks, step, 0)
    last_os, last_gates = out_dots(), [tg_scr[hd] for hd in heads]
    for hd in heads:
        out_finish(n_chunks - 1, last_os, last_gates, hd)


def _hgrn(hq, hl, hk, hv, hs, layer, nw):
    B, T, W = hq.shape
    row = lambda b, t: (b, t, 0)
    blk = pl.BlockSpec((1, HG_TB, W), row)
    C, NS = HG_CHUNK, HG_CHUNK // HG_SUB
    per_head = lambda *shape, dtype=F32: pltpu.VMEM((HG_HEADS,) + shape, dtype)
    return pl.pallas_call(
        _hgrn_kernel,
        grid=(B, T // HG_TB),
        in_specs=[blk, blk, blk, blk, blk,
                  pl.BlockSpec((None, 1, W), lambda b, t: (layer, 0, 0))],
        out_specs=pl.BlockSpec((1, HG_TB, W), row),
        out_shape=jax.ShapeDtypeStruct((B, T, W), BF16),
        scratch_shapes=[
            per_head(HG_V, HG_E),
            per_head(C, HG_E),
            per_head(C, HG_E),
            per_head(2 * NS, HG_E),
            per_head(NS - 1, NS, HG_E),
            per_head(C, HG_E),
            per_head(C, C, dtype=BF16),
            per_head(C, HG_V),
            per_head(C, HG_V, dtype=BF16),
            per_head(C, HG_V),
        ],
        compiler_params=_params(("parallel", "arbitrary")),
        name="hgrn2",
    )(hq, hl, hk, hv, hs, nw)


def _rot_cols(w):
    half = ROPE_DIM // 2
    parts = []
    for h in range(N_HEADS):
        base = h * HEAD_DIM
        parts += [-w[:, base + half:base + ROPE_DIM], w[:, base:base + half],
                  jnp.zeros((w.shape[0], HEAD_DIM - ROPE_DIM), w.dtype)]
    return jnp.concatenate(parts, axis=1)


W_IN_SIZES = (ATT_W, ATT_W, ATT_W, N_HEADS, ATT_W, ATT_W, ATT_W, HG_W, HG_W, HG_W, HG_W)
W_IN_ROWS = 256


def _relayout_kernel(w_ref, wa_ref, wb_ref):
    edges = np.cumsum((0,) + W_IN_SIZES)
    fq, fk, fv, ff, dq, dk, dv, hq, hf, hi, hg = (
        w_ref[:, int(lo):int(hi)] for lo, hi in zip(edges[:-1], edges[1:]))
    ff_pad = jnp.concatenate([ff, jnp.zeros((ff.shape[0], FF_LANES - N_HEADS), ff.dtype)], axis=1)
    wa_ref[...] = jnp.concatenate([fk, dk, _rot_cols(dk), ff_pad, hq, hf, hi, hg],
                                  axis=1).astype(BF16)
    wb_ref[...] = jnp.concatenate([fq, fv, dq, _rot_cols(dq), dv], axis=1).astype(BF16)


def _relayout_w_in(w_in):
    depth, rows, n_in = w_in.shape
    na = 3 * ATT_W + FF_LANES + 4 * HG_W
    nb = 5 * ATT_W
    blk = lambda n: pl.BlockSpec((None, W_IN_ROWS, n), lambda l, r: (l, r, 0))
    return pl.pallas_call(
        _relayout_kernel,
        grid=(depth, rows // W_IN_ROWS),
        in_specs=[blk(n_in)],
        out_specs=[blk(na), blk(nb)],
        out_shape=[jax.ShapeDtypeStruct((depth, rows, na), BF16),
                   jax.ShapeDtypeStruct((depth, rows, nb), BF16)],
        compiler_params=_params(("parallel", "parallel")),
        name="w_in_relayout",
    )(w_in)


def _rope_tables(positions):
    freqs = ROPE_THETA ** (-jnp.arange(0, ROPE_DIM, 2, dtype=F32) / ROPE_DIM)
    ang = positions.astype(F32)[:, :, None] * freqs
    B, T = positions.shape
    ones = jnp.ones((B, T, HEAD_DIM - ROPE_DIM), F32)
    cos_h = jnp.concatenate([jnp.cos(ang), jnp.cos(ang), ones], axis=-1)
    sin_h = jnp.concatenate([jnp.sin(ang), jnp.sin(ang), 0.0 * ones], axis=-1)
    return jnp.tile(cos_h, (1, 1, N_HEADS)), jnp.tile(sin_h, (1, 1, N_HEADS))


def kernel(x, positions, ffn1_norm, ffn1_w_gate, ffn1_w_up, ffn1_w_down, mix_norm, w_in,
           fox_forget_bias, hgrn_lower_bounds, hgrn_out_norm, w_out, ffn2_norm,
           ffn2_w_gate, ffn2_w_up, ffn2_w_down, final_norm):
    B, T, D = x.shape
    depth = w_in.shape[0]
    assert D == D_MODEL and T % max(TM_PROJ, HG_TB, BQ) == 0 and (B * T) % TM_FFN == 0
    assert BQ % BK == 0 and TM_PROJ % BK == 0

    sm = jax.nn.softmax(hgrn_lower_bounds.astype(F32), axis=0)
    lbs = (jnp.cumsum(sm, axis=0) - sm[0:1]).reshape(depth, 1, HG_W)
    wa, wb = _relayout_w_in(w_in)
    fb = jnp.pad(fox_forget_bias.astype(F32), ((0, 0), (0, FF_LANES - N_HEADS)))
    fb = fb.reshape(depth, 1, FF_LANES)
    stack_row = lambda p: p.astype(F32).reshape(depth, 1, p.shape[-1])
    n1, nm, n2, nh = (stack_row(p) for p in (ffn1_norm, mix_norm, ffn2_norm, hgrn_out_norm))
    w1 = (ffn1_w_gate, ffn1_w_up, ffn1_w_down)
    w2 = (ffn2_w_gate, ffn2_w_up, ffn2_w_down)
    wo = w_out
    fnorm = final_norm.astype(F32).reshape(1, D)

    cos, sin = _rope_tables(positions)
    dil_tab = _dilated_bias_table()

    xf = x.reshape(B * T, D)
    for i in range(depth):
        xf = _ffn(xf, i, n1, *w1)
        (fk_a, lf_a, dk_a, hq_a, hl_a, hk_a, hv_a, hs_a, fqt_a, fvt_a, dqt_a, dvt_a) = _inproj(
            xf.reshape(B, T, D), i, nm, wa, wb, cos, sin, fb, lbs)
        oa, ob = _attention(fqt_a, fk_a, fvt_a, lf_a, dqt_a, dk_a, dvt_a, dil_tab)
        oc = _hgrn(hq_a, hl_a, hk_a, hv_a, hs_a, i, nh)
        xf = _mix_ffn(xf, oa.reshape(B * T, ATT_W), ob.reshape(B * T, ATT_W),
                      oc.reshape(B * T, HG_W), i, wo, n2, *w2, fnorm, i == depth - 1)
    return xf.reshape(B, T, D)
```

```python
import functools
import math

import numpy as np
import jax
import jax.numpy as jnp
from jax import lax
from jax.experimental import pallas as pl
from jax.experimental.pallas import tpu as pltpu

F32 = jnp.float32
BF16 = jnp.bfloat16

D_MODEL = 1024
HEAD_DIM = 64
N_HEADS = 4
ATT_W = N_HEADS * HEAD_DIM
HG_HEADS = 4
HG_E = 128
HG_V = 128
HG_W = HG_HEADS * HG_V
ROPE_DIM = HEAD_DIM // 4
ROPE_THETA = 500000.0
D_FF = ((8 * D_MODEL // 3 + 127) // 128) * 128
EPS = 1e-6
NEG_BIG = -1e30
LB_FLOOR = 1e-30
DILATED_PATTERNS = ((128, 1), (512, 4), (2048, 16))
FF_LANES = 128
VT_ROWS = HEAD_DIM + 16
LOG2E = 1.0 / math.log(2.0)
QK_SCALE = HEAD_DIM ** -0.5 * LOG2E
N_SPLIT = 3

V7X_VMEM_BYTES = 64 * 1024 * 1024
VMEM_LIMIT = V7X_VMEM_BYTES - 8 * 1024 * 1024
SUBLANES = 8
LANES = 128

TM_FFN = 512
FF_CHUNK = 512
TM_PROJ = 512
BQ = 512
BK = 256
HG_CHUNK = 64
HG_SUB = SUBLANES
HG_TB = 1024
CS_BLK = 256


def _rms(x, g):
    ms = jnp.mean(x * x, axis=-1, keepdims=True)
    return x * lax.rsqrt(ms + EPS) * g


def _sigmoid(x):
    return 1.0 / (1.0 + jnp.exp(-x))


def _split3(x):
    hi = x.astype(BF16)
    rest = x - hi.astype(F32)
    mid = rest.astype(BF16)
    lo = (rest - mid.astype(F32)).astype(BF16)
    return hi, mid, lo


def _tril3(n):
    width = -(-N_SPLIT * n // LANES) * LANES
    r = lax.broadcasted_iota(jnp.int32, (n, width), 0)
    c = lax.broadcasted_iota(jnp.int32, (n, width), 1)
    return jnp.where((c % n <= r) & (c < N_SPLIT * n), 1.0, 0.0).astype(BF16)


def _cumsum_rows(tril3, pieces):
    pieces = list(pieces)
    rows, cols = pieces[0].shape
    pad = tril3.shape[1] - N_SPLIT * rows
    if pad:
        pieces.append(jnp.zeros((pad, cols), BF16))
    return jnp.dot(tril3, jnp.concatenate(pieces, axis=0), preferred_element_type=F32)


def _params(sem):
    return pltpu.CompilerParams(dimension_semantics=sem, vmem_limit_bytes=VMEM_LIMIT)


def _swiglu_residual(x, g_ref, wg_ref, wu_ref, wd_ref):
    h = _rms(x, g_ref[...]).astype(BF16)
    acc = jnp.zeros_like(x)
    for c in range(0, D_FF, FF_CHUNK):
        w = min(FF_CHUNK, D_FF - c)
        g = jnp.dot(h, wg_ref[:, c:c + w].astype(BF16), preferred_element_type=F32)
        u = jnp.dot(h, wu_ref[:, c:c + w].astype(BF16), preferred_element_type=F32)
        a = (g * _sigmoid(g) * u).astype(BF16)
        acc = acc + jnp.dot(a, wd_ref[c:c + w, :].astype(BF16), preferred_element_type=F32)
    return x + 0.5 * acc


def _ffn_kernel(x_ref, g_ref, wg_ref, wu_ref, wd_ref, o_ref):
    o_ref[...] = _swiglu_residual(x_ref[...], g_ref, wg_ref, wu_ref, wd_ref)


def _mix_ffn_kernel(final, x_ref, oa_ref, ob_ref, oc_ref, wo_ref, g_ref, wg_ref, wu_ref, wd_ref,
                    fg_ref, o_ref):
    W = ATT_W
    wo = lambda lo, hi: wo_ref[lo:hi, :].astype(BF16)
    x = x_ref[...] + jnp.dot(oa_ref[...], wo(0, W), preferred_element_type=F32)
    x = x + jnp.dot(ob_ref[...], wo(W, 2 * W), preferred_element_type=F32)
    x = x + jnp.dot(oc_ref[...], wo(2 * W, D_MODEL), preferred_element_type=F32)
    y = _swiglu_residual(x, g_ref, wg_ref, wu_ref, wd_ref)
    if final:
        y = _rms(y, fg_ref[...])
    o_ref[...] = y


def _layer_spec(arr, layer):
    return pl.BlockSpec((None,) + arr.shape[1:], lambda i: (layer, 0, 0),
                        pipeline_mode=pl.Buffered(1))


def _ffn(x2d, layer, g, wg, wu, wd):
    n = x2d.shape[0]
    row = lambda i: (i, 0)
    return pl.pallas_call(
        _ffn_kernel,
        grid=(n // TM_FFN,),
        in_specs=[pl.BlockSpec((TM_FFN, D_MODEL), row), _layer_spec(g, layer),
                  _layer_spec(wg, layer), _layer_spec(wu, layer), _layer_spec(wd, layer)],
        out_specs=pl.BlockSpec((TM_FFN, D_MODEL), row),
        out_shape=jax.ShapeDtypeStruct((n, D_MODEL), F32),
        compiler_params=_params(("parallel",)),
        name="ffn",
    )(x2d, g, wg, wu, wd)


def _mix_ffn(x2d, oa, ob, oc, layer, wo, g, wg, wu, wd, fg, final):
    n = x2d.shape[0]
    row = lambda i: (i, 0)
    return pl.pallas_call(
        functools.partial(_mix_ffn_kernel, final),
        grid=(n // TM_FFN,),
        in_specs=[pl.BlockSpec((TM_FFN, D_MODEL), row),
                  pl.BlockSpec((TM_FFN, ATT_W), row),
                  pl.BlockSpec((TM_FFN, ATT_W), row),
                  pl.BlockSpec((TM_FFN, HG_W), row),
                  _layer_spec(wo, layer), _layer_spec(g, layer),
                  _layer_spec(wg, layer), _layer_spec(wu, layer), _layer_spec(wd, layer),
                  pl.BlockSpec((1, D_MODEL), lambda i: (0, 0))],
        out_specs=pl.BlockSpec((TM_FFN, D_MODEL), row),
        out_shape=jax.ShapeDtypeStruct((n, D_MODEL), F32),
        compiler_params=_params(("parallel",)),
        name="mix_ffn_final" if final else "mix_ffn",
    )(x2d, oa, ob, oc, wo, g, wg, wu, wd, fg)


def _inproj_kernel(x_ref, g_ref, wa_ref, wb_ref, cos_ref, sin_ref, fb_ref, lb_ref,
                   fk_ref, lf_ref, dk_ref, hq_ref, hl_ref, hk_ref, hv_ref, hs_ref,
                   fqt_ref, fvt_ref, dqt_ref, dvt_ref):
    h = _rms(x_ref[0], g_ref[...]).astype(BF16)
    W = ATT_W
    pa = jnp.dot(h, wa_ref[:, 0:3 * W + FF_LANES], preferred_element_type=F32)
    fk_ref[0] = pa[:, 0:W].astype(BF16)
    dk_ref[0] = (pa[:, W:2 * W] * cos_ref[0] + pa[:, 2 * W:3 * W] * sin_ref[0]).astype(BF16)
    z = pa[:, 3 * W:3 * W + FF_LANES] + fb_ref[...]
    lf_ref[0] = (jnp.minimum(z, 0.0) - jnp.log1p(jnp.exp(-jnp.abs(z)))) * LOG2E

    base = 3 * W + FF_LANES
    hdot = lambda i: jnp.dot(h, wa_ref[:, base + i * HG_W:base + (i + 1) * HG_W],
                             preferred_element_type=F32)
    q = hdot(0)
    hq_ref[0] = q * _sigmoid(q)
    z = hdot(1)
    lbf = jnp.clip(lb_ref[...], 0.0, 1.0 - 1e-6)
    ez = jnp.exp(-jnp.abs(z))
    inv = 1.0 / (1.0 + ez)
    sig_pos = jnp.where(z >= 0, inv, ez * inv)
    sig_neg = jnp.where(z >= 0, ez * inv, inv)
    hl_ref[0] = jnp.log(jnp.maximum(lbf, LB_FLOOR) + (1.0 - lbf) * sig_pos) * LOG2E
    hk_ref[0] = (1.0 - lbf) * sig_neg
    hv_ref[0] = hdot(2)
    hs_ref[0] = _sigmoid(hdot(3))

    pb = jnp.dot(h, wb_ref[...], preferred_element_type=F32)
    fqt_ref[0] = (pb[:, 0:W] * QK_SCALE).T.astype(BF16)
    dq = (pb[:, 2 * W:3 * W] * cos_ref[0] + pb[:, 3 * W:4 * W] * sin_ref[0]) * QK_SCALE
    dqt_ref[0] = dq.T.astype(BF16)
    ones = jnp.ones((VT_ROWS - HEAD_DIM, BK), BF16)
    for src, ref in ((pb[:, W:2 * W].T.astype(BF16), fvt_ref),
                     (pb[:, 4 * W:5 * W].T.astype(BF16), dvt_ref)):
        for j in range(TM_PROJ // BK):
            ref[0, j] = jnp.concatenate(
                [part for h in range(N_HEADS)
                 for part in (src[h * HEAD_DIM:(h + 1) * HEAD_DIM, j * BK:(j + 1) * BK], ones)],
                axis=0)


def _inproj(x, layer, g, wa, wb, cos, sin, fb, lb):
    B, T, _ = x.shape
    W = ATT_W
    nt = T // TM_PROJ
    per_layer = lambda arr: pl.BlockSpec((None,) + arr.shape[1:], lambda b, t: (layer, 0, 0),
                                         pipeline_mode=pl.Buffered(1))
    row = lambda b, t: (b, t, 0)
    col = lambda b, t: (b, 0, t)
    blk4 = lambda b, t: (b, t, 0, 0)
    hg_shape = jax.ShapeDtypeStruct((B, T, HG_W), F32)
    hg_spec = pl.BlockSpec((1, TM_PROJ, HG_W), row)
    out_shape = [
        jax.ShapeDtypeStruct((B, T, W), BF16),
        jax.ShapeDtypeStruct((B, T, FF_LANES), F32),
        jax.ShapeDtypeStruct((B, T, W), BF16),
        hg_shape, hg_shape, hg_shape, hg_shape, hg_shape,
        jax.ShapeDtypeStruct((B, W, T), BF16),
        jax.ShapeDtypeStruct((B, T // BK, N_HEADS * VT_ROWS, BK), BF16),
        jax.ShapeDtypeStruct((B, W, T), BF16),
        jax.ShapeDtypeStruct((B, T // BK, N_HEADS * VT_ROWS, BK), BF16),
    ]
    out_specs = [
        pl.BlockSpec((1, TM_PROJ, W), row),
        pl.BlockSpec((1, TM_PROJ, FF_LANES), row),
        pl.BlockSpec((1, TM_PROJ, W), row),
        hg_spec, hg_spec, hg_spec, hg_spec, hg_spec,
        pl.BlockSpec((1, W, TM_PROJ), col),
        pl.BlockSpec((1, TM_PROJ // BK, N_HEADS * VT_ROWS, BK), blk4),
        pl.BlockSpec((1, W, TM_PROJ), col),
        pl.BlockSpec((1, TM_PROJ // BK, N_HEADS * VT_ROWS, BK), blk4),
    ]
    return pl.pallas_call(
        _inproj_kernel,
        grid=(B, nt),
        in_specs=[
            pl.BlockSpec((1, TM_PROJ, D_MODEL), row),
            per_layer(g), per_layer(wa), per_layer(wb),
            pl.BlockSpec((1, TM_PROJ, W), row),
            pl.BlockSpec((1, TM_PROJ, W), row),
            per_layer(fb), per_layer(lb),
        ],
        out_specs=out_specs,
        out_shape=out_shape,
        compiler_params=_params(("parallel", "parallel")),
        name="inproj",
    )(x, g, wa, wb, cos, sin, fb, lb)


def _attn_pipeline(fox, qi, qt_ref, k_ref, vt_ref, aux_ref, sel_ref, o_ref,
                   qm_ref, acc_ref, m_ref, st_slots, ka_ref):
    T = k_ref.shape[1]
    R = BQ // BK
    assert R == 2
    qt = qt_ref[0]

    if fox:
        @pl.when(qi == 0)
        def _():
            tril3 = _tril3(CS_BLK)
            blocks = [slice(i * CS_BLK, (i + 1) * CS_BLK) for i in range(T // CS_BLK)]
            local = [_cumsum_rows(tril3, _split3(aux_ref[0, rows, :])) for rows in blocks]
            carry = jnp.zeros((1, FF_LANES), F32)
            for rows, loc in zip(blocks, local):
                blk = loc + carry
                carry = blk[CS_BLK - 1:CS_BLK, :]
                cat = jnp.concatenate([k_ref[0, rows, :], *_split3(blk)], axis=1)
                ka_ref[rows, :] = jnp.dot(cat, sel_ref[...],
                                          preferred_element_type=F32).astype(BF16)

        pad_row = lax.broadcasted_iota(jnp.int32, (HEAD_DIM, BQ), 0)
        minus_ones = jnp.where(pad_row < N_SPLIT, -1.0, 0.0).astype(BF16)
        for h in range(N_HEADS):
            qm_ref[h] = jnp.concatenate([qt[h * HEAD_DIM:(h + 1) * HEAD_DIM], minus_ones], axis=0)
    else:
        head_of_row = lax.broadcasted_iota(jnp.int32, qt.shape, 0) // HEAD_DIM
        for h in range(N_HEADS):
            qm_ref[h] = jnp.where(head_of_row == h, qt, jnp.zeros_like(qt))
    acc_ref[...] = jnp.zeros_like(acc_ref)
    m_ref[...] = jnp.full(m_ref.shape, NEG_BIG, F32)
    krow = lax.broadcasted_iota(jnp.int32, (BK, BQ), 0)
    qcol = lax.broadcasted_iota(jnp.int32, (BK, BQ), 1)
    tri = (lax.broadcasted_iota(jnp.int32, (BK, BK), 0)
           <= lax.broadcasted_iota(jnp.int32, (BK, BK), 1))
    upper = slice(BK, BQ)

    def fetch(j, last=False):
        start = pl.multiple_of(j * BK, BK)
        if fox:
            kbs = [ka_ref[pl.ds(start, BK), 2 * h * HEAD_DIM:2 * (h + 1) * HEAD_DIM]
                   for h in range(N_HEADS)]
            aux = None
        else:
            kbs = [k_ref[0, pl.ds(start, BK), :]] * N_HEADS
            halves = (1,) if last else range(R)
            aux = jnp.concatenate([aux_ref[R * qi + half - j] for half in halves], axis=1)
        vbs = [vt_ref[0, j, h * VT_ROWS:(h + 1) * VT_ROWS, :] for h in range(N_HEADS)]
        return kbs, vbs, aux

    def head_stage(h, blk, slot, nxt=None, nxt_last=False, diag=False, last=False):
        _, vbs, aux = blk
        cols = upper if last else slice(0, BQ)
        nxt_cols = upper if nxt_last else slice(0, BQ)
        if nxt is not None:
            st_slots[1 - slot][h, :, nxt_cols] = jnp.dot(
                nxt[0][h], qm_ref[h, :, nxt_cols], preferred_element_type=F32)
        st = st_slots[slot][h, :, cols]
        if not fox:
            st = st + aux
        elif last:
            st = jnp.where(tri, st, NEG_BIG)
        elif diag:
            st = jnp.where(krow <= qcol, st, NEG_BIG)
        m_old = m_ref[h, :, cols]
        m_new = jnp.maximum(m_old, jnp.max(st, axis=0, keepdims=True))
        m_ref[h, :, cols] = m_new
        p = jnp.exp2(st - m_new)
        alpha = jnp.exp2(m_old - m_new)
        rows = slice(h * VT_ROWS, (h + 1) * VT_ROWS)
        acc_ref[rows, cols] = alpha * acc_ref[rows, cols] + jnp.dot(
            vbs[h], p.astype(BF16), preferred_element_type=F32)

    def first_scores(h, blk):
        st_slots[0][h] = jnp.dot(blk[0][h], qm_ref[h], preferred_element_type=F32)

    def finish():
        outs = []
        for h in range(N_HEADS):
            num = acc_ref[h * VT_ROWS:h * VT_ROWS + HEAD_DIM, :]
            den = acc_ref[h * VT_ROWS + HEAD_DIM:h * VT_ROWS + HEAD_DIM + 1, :]
            outs.append(num / den)
        o_ref[0] = jnp.concatenate(outs, axis=0).T.astype(BF16)

    return fetch, head_stage, first_scores, finish


def _attn_kernel(fqt_ref, fk_ref, fvt_ref, lf_ref, sel_ref, dqt_ref, dk_ref, dvt_ref, tab_ref,
                 of_ref, od_ref,
                 fqm, facc, fm, fst0, fst1, fka, dqm, dacc, dm, dst0, dst1):
    qi = pl.program_id(1)
    R = BQ // BK
    groups = (
        _attn_pipeline(True, qi, fqt_ref, fk_ref, fvt_ref, lf_ref, sel_ref, of_ref,
                       fqm, facc, fm, (fst0, fst1), fka),
        _attn_pipeline(False, qi, dqt_ref, dk_ref, dvt_ref, tab_ref, None, od_ref,
                       dqm, dacc, dm, (dst0, dst1), None),
    )

    def stage(blks, slot, nxts=None, **flags):
        for h in range(N_HEADS):
            for g, (_, head_stage, _, _) in enumerate(groups):
                head_stage(h, blks[g], slot, nxt=None if nxts is None else nxts[g], **flags)

    def fetch_all(j, **kw):
        return [fetch(j, **kw) for fetch, _, _, _ in groups]

    def pair(i, _):
        b0, b1, b2 = fetch_all(2 * i), fetch_all(2 * i + 1), fetch_all(2 * i + 2)
        stage(b0, 0, b1)
        stage(b1, 1, b2)
        return 0

    first = fetch_all(0)
    for h in range(N_HEADS):
        for g, (_, _, first_scores, _) in enumerate(groups):
            first_scores(h, first[g])
    lax.fori_loop(0, qi, pair, 0)
    b0, b1 = fetch_all(R * qi), fetch_all(R * qi + 1, last=True)
    stage(b0, 0, b1, nxt_last=True, diag=True)
    stage(b1, 1, last=True)
    for _, _, _, finish in groups:
        finish()


def _fox_slot_matrix():
    sel = np.zeros((ATT_W + N_SPLIT * FF_LANES, N_HEADS * 2 * HEAD_DIM), np.float32)
    for h in range(N_HEADS):
        for d in range(HEAD_DIM):
            sel[h * HEAD_DIM + d, 2 * h * HEAD_DIM + d] = 1.0
        for piece in range(N_SPLIT):
            sel[ATT_W + piece * FF_LANES + h, (2 * h + 1) * HEAD_DIM + piece] = 1.0
    return jnp.asarray(sel, BF16)


def _attention(fqt, fk, fvt, lf, dqt, dk, dvt, tab):
    B, T, W = fk.shape
    sel = _fox_slot_matrix()
    q_spec = pl.BlockSpec((1, W, BQ), lambda b, q: (b, 0, q))
    k_spec = pl.BlockSpec((1, T, W), lambda b, q: (b, 0, 0))
    v_spec = pl.BlockSpec((1, T // BK, N_HEADS * VT_ROWS, BK), lambda b, q: (b, 0, 0, 0))
    o_spec = pl.BlockSpec((1, BQ, W), lambda b, q: (b, q, 0))
    o_shape = jax.ShapeDtypeStruct((B, T, W), BF16)

    def group_scratch(q_rows):
        return [pltpu.VMEM((N_HEADS, q_rows, BQ), BF16),
                pltpu.VMEM((N_HEADS * VT_ROWS, BQ), F32),
                pltpu.VMEM((N_HEADS, 1, BQ), F32),
                pltpu.VMEM((N_HEADS, BK, BQ), F32), pltpu.VMEM((N_HEADS, BK, BQ), F32)]

    return pl.pallas_call(
        _attn_kernel,
        grid=(B, T // BQ),
        in_specs=[q_spec, k_spec, v_spec,
                  pl.BlockSpec((1, T, FF_LANES), lambda b, q: (b, 0, 0)),
                  pl.BlockSpec(sel.shape, lambda b, q: (0, 0)),
                  q_spec, k_spec, v_spec,
                  pl.BlockSpec(tab.shape, lambda b, q: (0, 0, 0))],
        out_specs=[o_spec, o_spec],
        out_shape=[o_shape, o_shape],
        scratch_shapes=(group_scratch(2 * HEAD_DIM)
                        + [pltpu.VMEM((T, N_HEADS * 2 * HEAD_DIM), BF16)]
                        + group_scratch(W)),
        compiler_params=_params(("parallel", "arbitrary")),
        name="attention",
    )(fqt, fk, fvt, lf, sel, dqt, dk, dvt, tab)


def _dilated_bias_table():
    nd = max(w for w, _ in DILATED_PATTERNS) // BK
    d = (np.arange(nd)[:, None, None] * BK + np.arange(BK)[None, None, :]
         - np.arange(BK)[None, :, None])
    mult = np.zeros(d.shape, np.float64)
    for window, dil in DILATED_PATTERNS:
        mult += (d >= 0) & (d <= window) & (d % dil == 0)
    tab = np.where(mult > 0, np.log2(np.maximum(mult, 1.0)), NEG_BIG)
    return jnp.asarray(tab, F32)


def _hgrn_kernel(q_ref, lf_ref, k_ref, v_ref, s_ref, nw_ref, o_ref,
                 st_ref, b_scr, k_scr, a_scr, g_scr, bn_scr, ta_scr, ti_scr, tv_scr, tg_scr):
    C, S, E = HG_CHUNK, HG_SUB, HG_E
    NS = C // S
    n_chunks = HG_TB // C
    heads = range(HG_HEADS)
    lanes = [slice(hd * E, (hd + 1) * E) for hd in heads]

    @pl.when(pl.program_id(1) == 0)
    def _():
        st_ref[...] = jnp.zeros_like(st_ref)

    r = lax.broadcasted_iota(jnp.int32, (C, C), 0)
    c = lax.broadcasted_iota(jnp.int32, (C, C), 1)
    tril3 = _tril3(C)
    diag01 = jnp.where(((c // S) == (r // S)) & (c <= r), 1.0, 0.0)
    rr = lax.broadcasted_iota(jnp.int32, (S * E, C), 0)
    rc = lax.broadcasted_iota(jnp.int32, (S * E, C), 1)
    red = jnp.where(rr // E == rc % S, 1.0, 0.0).astype(BF16)
    not_first01 = jnp.where(lax.broadcasted_iota(jnp.int32, (NS, E), 0) == 0, 0.0, 1.0)
    zeros_sub = jnp.zeros((S, E), F32)

    def rows_bcast(ref, hd, idx):
        return jnp.concatenate(
            [jnp.broadcast_to(ref[hd, idx(i):idx(i) + 1, :], (S, E)) for i in range(NS)], axis=0)

    def cumsum(ci):
        r0 = pl.multiple_of(ci * C, C)
        return [_cumsum_rows(tril3, _split3(lf_ref[0, pl.ds(r0, C), lanes[hd]])) for hd in heads]

    def out_dots():
        return [ti_scr[hd] + jnp.dot(ta_scr[hd], tv_scr[hd], preferred_element_type=F32)
                for hd in heads]

    def out_finish(ci, os_, gates, hd):
        r0 = pl.multiple_of(ci * C, C)
        o = os_[hd]
        o = o * lax.rsqrt(jnp.mean(o * o, axis=-1, keepdims=True) + EPS)
        o_ref[0, pl.ds(r0, C), lanes[hd]] = (o * gates[hd]).astype(BF16)

    def scores(ci, bs_, after_head):
        r0 = pl.multiple_of(ci * C, C)
        for hd in heads:
            b = bs_[hd]
            q = q_ref[0, pl.ds(r0, C), lanes[hd]]
            k = k_ref[0, pl.ds(r0, C), lanes[hd]]
            v = v_ref[0, pl.ds(r0, C), lanes[hd]]
            b_last = b[C - 1:C, :]
            st = st_ref[hd]
            qe = (q * jnp.exp2(b)).astype(BF16)
            ti_scr[hd] = lax.dot_general(qe, st.astype(BF16), (((1,), (1,)), ((), ())),
                                         preferred_element_type=F32)
            kdec = (k * jnp.exp2(b_last - b)).astype(BF16)
            st_ref[hd] = st * jnp.exp2(b_last) + jnp.dot(
                v.T.astype(BF16), kdec, preferred_element_type=F32)
            tv_scr[hd] = v.astype(BF16)
            tg_scr[hd] = s_ref[0, pl.ds(r0, C), lanes[hd]] * nw_ref[:, lanes[hd]]
            b_scr[hd] = b
            k_scr[hd] = k
            b_end = b_scr[hd, pl.ds(S - 1, NS, stride=S), :]
            b_prev = pltpu.roll(b_end, 1, axis=0) * not_first01
            a_scr[hd, 0:NS] = b_end
            a_scr[hd, NS:2 * NS] = b_prev
            q_t = q * jnp.exp2(b - rows_bcast(a_scr, hd, lambda i: NS + i))
            k_h = k * jnp.exp2(rows_bcast(a_scr, hd, lambda i: i) - b)
            for j in range(NS - 1):
                g_scr[hd, j] = jnp.exp2(jnp.minimum(b_prev - b_end[j:j + 1, :], 0.0))
            lhs_rows, rhs_rows = [], []
            for i in range(NS):
                qi_t = q_t[i * S:(i + 1) * S, :]
                lhs_rows.append(jnp.concatenate(
                    [qi_t * g_scr[hd, j, i:i + 1, :] if i > j else zeros_sub
                     for j in range(NS - 1)], axis=1))
                rhs_rows.append(jnp.concatenate(
                    [k_h[i * S:(i + 1) * S, :] if i == j else zeros_sub
                     for j in range(NS - 1)], axis=1))
            a_off = lax.dot_general(jnp.concatenate(lhs_rows, axis=0).astype(BF16),
                                    jnp.concatenate(rhs_rows, axis=0).astype(BF16),
                                    (((1,), (1,)), ((), ())),
                                    preferred_element_type=F32)
            w_parts = []
            for s in range(S):
                bs = rows_bcast(b_scr, hd, lambda i: i * S + s)
                ks = rows_bcast(k_scr, hd, lambda i: i * S + s)
                w_parts.append((jnp.exp2(jnp.minimum(b - bs, 0.0)) * (q * ks)).astype(BF16))
            a_diag = jnp.dot(jnp.concatenate(w_parts, axis=1), red,
                             preferred_element_type=F32)
            ta_scr[hd] = (a_off + a_diag * diag01).astype(BF16)
            after_head(hd)

    def step(ci, _):
        bs_ = [bn_scr[hd] for hd in heads]
        gates = [tg_scr[hd] for hd in heads]
        nxt = cumsum(jnp.minimum(ci + 1, n_chunks - 1))
        os_ = out_dots()
        scores(ci, bs_, functools.partial(out_finish, jnp.maximum(ci - 1, 0), os_, gates))
        for hd in heads:
            bn_scr[hd] = nxt[hd]
        return 0

    for ref in (ta_scr, ti_scr, tv_scr, tg_scr):
        ref[...] = jnp.zeros_like(ref)
    first = cumsum(0)
    for hd in heads:
        bn_scr[hd] = first[hd]
    lax.fori_loop(0, n_chunks, step, 0)
    last_os, last_gates = out_dots(), [tg_scr[hd] for hd in heads]
    for hd in heads:
        out_finish(n_chunks - 1, last_os, last_gates, hd)


def _hgrn(hq, hl, hk, hv, hs, layer, nw):
    B, T, W = hq.shape
    row = lambda b, t: (b, t, 0)
    blk = pl.BlockSpec((1, HG_TB, W), row)
    C, NS = HG_CHUNK, HG_CHUNK // HG_SUB
    per_head = lambda *shape, dtype=F32: pltpu.VMEM((HG_HEADS,) + shape, dtype)
    return pl.pallas_call(
        _hgrn_kernel,
        grid=(B, T // HG_TB),
        in_specs=[blk, blk, blk, blk, blk,
                  pl.BlockSpec((None, 1, W), lambda b, t: (layer, 0, 0))],
        out_specs=pl.BlockSpec((1, HG_TB, W), row),
        out_shape=jax.ShapeDtypeStruct((B, T, W), BF16),
        scratch_shapes=[
            per_head(HG_V, HG_E),
            per_head(C, HG_E),
            per_head(C, HG_E),
            per_head(2 * NS, HG_E),
            per_head(NS - 1, NS, HG_E),
            per_head(C, HG_E),
            per_head(C, C, dtype=BF16),
            per_head(C, HG_V),
            per_head(C, HG_V, dtype=BF16),
            per_head(C, HG_V),
        ],
        compiler_params=_params(("parallel", "arbitrary")),
        name="hgrn2",
    )(hq, hl, hk, hv, hs, nw)


W_IN_SIZES = (ATT_W, ATT_W, ATT_W, N_HEADS, ATT_W, ATT_W, ATT_W, HG_W, HG_W, HG_W, HG_W)
COPY, ROTARY, FORGET = 0, 1, 2


def _relayout_plan():
    fq, fk, fv, ff, dq, dk, dv, hq, hf, hi, hg = (int(e) for e in np.cumsum((0,) + W_IN_SIZES)[:-1])
    span = lambda start, width, kind=COPY: [(start + o, kind) for o in range(0, width, LANES)]
    wa = (span(fk, ATT_W) + span(dk, ATT_W) + span(dk, ATT_W, ROTARY) + [(ff, FORGET)]
          + span(hq, 4 * HG_W))
    wb = (span(fq, ATT_W) + span(fv, ATT_W) + span(dq, ATT_W) + span(dq, ATT_W, ROTARY)
          + span(dv, ATT_W))
    return wa, wb


def _relayout_kernel(start_ref, kind_ref, w_ref, o_ref):
    del start_ref
    kind = kind_ref[pl.program_id(0)]
    half = ROPE_DIM // 2
    row = lax.broadcasted_iota(jnp.int32, (LANES, D_MODEL), 0)
    for layer in range(o_ref.shape[0]):
        x = w_ref[:, layer, :]
        zeros = jnp.zeros((HEAD_DIM - ROPE_DIM, D_MODEL), F32)
        rot = jnp.concatenate(
            [part for h in range(LANES // HEAD_DIM)
             for part in (-x[h * HEAD_DIM + half:h * HEAD_DIM + ROPE_DIM],
                          x[h * HEAD_DIM:h * HEAD_DIM + half], zeros)], axis=0)
        forget = jnp.where(row < N_HEADS, x, 0.0)
        y = jnp.where(kind == ROTARY, rot, jnp.where(kind == FORGET, forget, x))
        o_ref[layer] = y.T.astype(BF16)


def _relayout_w_in(w_in):
    depth, d_model, _ = w_in.shape
    w_t = w_in.transpose(2, 0, 1)
    outs = []
    for plan in _relayout_plan():
        starts = jnp.asarray([s for s, _ in plan], jnp.int32)
        kinds = jnp.asarray([k for _, k in plan], jnp.int32)
        outs.append(pl.pallas_call(
            _relayout_kernel,
            grid_spec=pltpu.PrefetchScalarGridSpec(
                num_scalar_prefetch=2,
                grid=(len(plan),),
                in_specs=[pl.BlockSpec(
                    (pl.Element(LANES), pl.Element(depth), pl.Element(d_model)),
                    lambda g, starts, kinds: (starts[g], 0, 0))],
                out_specs=pl.BlockSpec((depth, d_model, LANES), lambda g, starts, kinds: (0, 0, g)),
            ),
            out_shape=jax.ShapeDtypeStruct((depth, d_model, len(plan) * LANES), BF16),
            compiler_params=_params(("parallel",)),
            name="w_in_relayout",
        )(starts, kinds, w_t))
    return outs


def _rope_tables(positions):
    freqs = ROPE_THETA ** (-jnp.arange(0, ROPE_DIM, 2, dtype=F32) / ROPE_DIM)
    ang = positions.astype(F32)[:, :, None] * freqs
    B, T = positions.shape
    ones = jnp.ones((B, T, HEAD_DIM - ROPE_DIM), F32)
    cos_h = jnp.concatenate([jnp.cos(ang), jnp.cos(ang), ones], axis=-1)
    sin_h = jnp.concatenate([jnp.sin(ang), jnp.sin(ang), 0.0 * ones], axis=-1)
    return jnp.tile(cos_h, (1, 1, N_HEADS)), jnp.tile(sin_h, (1, 1, N_HEADS))


def kernel(x, positions, ffn1_norm, ffn1_w_gate, ffn1_w_up, ffn1_w_down, mix_norm, w_in,
           fox_forget_bias, hgrn_lower_bounds, hgrn_out_norm, w_out, ffn2_norm,
           ffn2_w_gate, ffn2_w_up, ffn2_w_down, final_norm):
    B, T, D = x.shape
    depth = w_in.shape[0]
    assert D == D_MODEL and T % max(TM_PROJ, HG_TB, BQ) == 0 and (B * T) % TM_FFN == 0
    assert BQ % BK == 0 and TM_PROJ % BK == 0

    sm = jax.nn.softmax(hgrn_lower_bounds.astype(F32), axis=0)
    lbs = (jnp.cumsum(sm, axis=0) - sm[0:1]).reshape(depth, 1, HG_W)
    wa, wb = _relayout_w_in(w_in)
    fb = jnp.pad(fox_forget_bias.astype(F32), ((0, 0), (0, FF_LANES - N_HEADS)))
    fb = fb.reshape(depth, 1, FF_LANES)
    stack_row = lambda p: p.astype(F32).reshape(depth, 1, p.shape[-1])
    n1, nm, n2, nh = (stack_row(p) for p in (ffn1_norm, mix_norm, ffn2_norm, hgrn_out_norm))
    w1 = (ffn1_w_gate, ffn1_w_up, ffn1_w_down)
    w2 = (ffn2_w_gate, ffn2_w_up, ffn2_w_down)
    wo = w_out
    fnorm = final_norm.astype(F32).reshape(1, D)

    cos, sin = _rope_tables(positions)
    dil_tab = _dilated_bias_table()

    xf = x.reshape(B * T, D)
    for i in range(depth):
        xf = _ffn(xf, i, n1, *w1)
        (fk_a, lf_a, dk_a, hq_a, hl_a, hk_a, hv_a, hs_a, fqt_a, fvt_a, dqt_a, dvt_a) = _inproj(
            xf.reshape(B, T, D), i, nm, wa, wb, cos, sin, fb, lbs)
        oa, ob = _attention(fqt_a, fk_a, fvt_a, lf_a, dqt_a, dk_a, dvt_a, dil_tab)
        oc = _hgrn(hq_a, hl_a, hk_a, hv_a, hs_a, i, nh)
        xf = _mix_ffn(xf, oa.reshape(B * T, ATT_W), ob.reshape(B * T, ATT_W),
                      oc.reshape(B * T, HG_W), i, wo, n2, *w2, fnorm, i == depth - 1)
    return xf.reshape(B, T, D)
```

```python
import functools
import math

import numpy as np
import jax
import jax.numpy as jnp
from jax import lax
from jax.experimental import pallas as pl
from jax.experimental.pallas import tpu as pltpu

F32 = jnp.float32
BF16 = jnp.bfloat16

D_MODEL = 1024
HEAD_DIM = 64
N_HEADS = 4
ATT_W = N_HEADS * HEAD_DIM
HG_HEADS = 4
HG_E = 128
HG_V = 128
HG_W = HG_HEADS * HG_V
ROPE_DIM = HEAD_DIM // 4
ROPE_THETA = 500000.0
D_FF = ((8 * D_MODEL // 3 + 127) // 128) * 128
EPS = 1e-6
NEG_BIG = -1e30
LB_FLOOR = 1e-30
DILATED_PATTERNS = ((128, 1), (512, 4), (2048, 16))
FF_LANES = 128
VT_ROWS = HEAD_DIM + 16
LOG2E = 1.0 / math.log(2.0)
QK_SCALE = HEAD_DIM ** -0.5 * LOG2E
N_SPLIT = 3
ROT_LANES = 128
REST_DIM = HEAD_DIM - ROPE_DIM

V7X_VMEM_BYTES = 64 * 1024 * 1024
VMEM_LIMIT = V7X_VMEM_BYTES - 8 * 1024 * 1024
SUBLANES = 8
LANES = 128

TM_FFN = 512
FF_CHUNK = 512
TM_PROJ = 512
BQ = 512
BK = 256
HG_CHUNK = 64
HG_SUB = SUBLANES
HG_TB = 1024
CS_BLK = 256


def _rms(x, g):
    ms = jnp.mean(x * x, axis=-1, keepdims=True)
    return x * lax.rsqrt(ms + EPS) * g


def _sigmoid(x):
    return 1.0 / (1.0 + jnp.exp(-x))


def _split3(x):
    hi = x.astype(BF16)
    rest = x - hi.astype(F32)
    mid = rest.astype(BF16)
    lo = (rest - mid.astype(F32)).astype(BF16)
    return hi, mid, lo


def _tril3(n):
    width = -(-N_SPLIT * n // LANES) * LANES
    r = lax.broadcasted_iota(jnp.int32, (n, width), 0)
    c = lax.broadcasted_iota(jnp.int32, (n, width), 1)
    return jnp.where((c % n <= r) & (c < N_SPLIT * n), 1.0, 0.0).astype(BF16)


def _cumsum_rows(tril3, pieces):
    pieces = list(pieces)
    rows, cols = pieces[0].shape
    pad = tril3.shape[1] - N_SPLIT * rows
    if pad:
        pieces.append(jnp.zeros((pad, cols), BF16))
    return jnp.dot(tril3, jnp.concatenate(pieces, axis=0), preferred_element_type=F32)


def _params(sem):
    return pltpu.CompilerParams(dimension_semantics=sem, vmem_limit_bytes=VMEM_LIMIT)


def _swiglu_residual(x, g_ref, wg_ref, wu_ref, wd_ref):
    h = _rms(x, g_ref[...]).astype(BF16)
    acc = jnp.zeros_like(x)
    for c in range(0, D_FF, FF_CHUNK):
        w = min(FF_CHUNK, D_FF - c)
        g = jnp.dot(h, wg_ref[:, c:c + w].astype(BF16), preferred_element_type=F32)
        u = jnp.dot(h, wu_ref[:, c:c + w].astype(BF16), preferred_element_type=F32)
        a = (g * _sigmoid(g) * u).astype(BF16)
        acc = acc + jnp.dot(a, wd_ref[c:c + w, :].astype(BF16), preferred_element_type=F32)
    return x + 0.5 * acc


def _ffn_kernel(x_ref, g_ref, wg_ref, wu_ref, wd_ref, o_ref):
    o_ref[...] = _swiglu_residual(x_ref[...], g_ref, wg_ref, wu_ref, wd_ref)


def _mix_ffn_kernel(final, x_ref, oa_ref, ob_ref, oc_ref, wo_ref, g_ref, wg_ref, wu_ref, wd_ref,
                    fg_ref, o_ref):
    W = ATT_W
    wo = lambda lo, hi: wo_ref[lo:hi, :].astype(BF16)
    x = x_ref[...] + jnp.dot(oa_ref[...], wo(0, W), preferred_element_type=F32)
    x = x + jnp.dot(ob_ref[...], wo(W, 2 * W), preferred_element_type=F32)
    x = x + jnp.dot(oc_ref[...], wo(2 * W, D_MODEL), preferred_element_type=F32)
    y = _swiglu_residual(x, g_ref, wg_ref, wu_ref, wd_ref)
    if final:
        y = _rms(y, fg_ref[...])
    o_ref[...] = y


def _layer_spec(arr, layer):
    return pl.BlockSpec((None,) + arr.shape[1:], lambda i: (layer, 0, 0),
                        pipeline_mode=pl.Buffered(1))


def _ffn(x2d, layer, g, wg, wu, wd):
    n = x2d.shape[0]
    row = lambda i: (i, 0)
    return pl.pallas_call(
        _ffn_kernel,
        grid=(n // TM_FFN,),
        in_specs=[pl.BlockSpec((TM_FFN, D_MODEL), row), _layer_spec(g, layer),
                  _layer_spec(wg, layer), _layer_spec(wu, layer), _layer_spec(wd, layer)],
        out_specs=pl.BlockSpec((TM_FFN, D_MODEL), row),
        out_shape=jax.ShapeDtypeStruct((n, D_MODEL), F32),
        compiler_params=_params(("parallel",)),
        name="ffn",
    )(x2d, g, wg, wu, wd)


def _mix_ffn(x2d, oa, ob, oc, layer, wo, g, wg, wu, wd, fg, final):
    n = x2d.shape[0]
    row = lambda i: (i, 0)
    return pl.pallas_call(
        functools.partial(_mix_ffn_kernel, final),
        grid=(n // TM_FFN,),
        in_specs=[pl.BlockSpec((TM_FFN, D_MODEL), row),
                  pl.BlockSpec((TM_FFN, ATT_W), row),
                  pl.BlockSpec((TM_FFN, ATT_W), row),
                  pl.BlockSpec((TM_FFN, HG_W), row),
                  _layer_spec(wo, layer), _layer_spec(g, layer),
                  _layer_spec(wg, layer), _layer_spec(wu, layer), _layer_spec(wd, layer),
                  pl.BlockSpec((1, D_MODEL), lambda i: (0, 0))],
        out_specs=pl.BlockSpec((TM_FFN, D_MODEL), row),
        out_shape=jax.ShapeDtypeStruct((n, D_MODEL), F32),
        compiler_params=_params(("parallel",)),
        name="mix_ffn_final" if final else "mix_ffn",
    )(x2d, oa, ob, oc, wo, g, wg, wu, wd, fg)


def _inproj_kernel(x_ref, g_ref, wa_ref, wb_ref, cos_ref, sin_ref, fb_ref, lb_ref,
                   fk_ref, lf_ref, dk_ref, hq_ref, hl_ref, hk_ref, hv_ref, hs_ref,
                   fqt_ref, fvt_ref, dqt_ref, dvt_ref):
    h = _rms(x_ref[0], g_ref[...]).astype(BF16)
    W = ATT_W
    R = ROT_LANES
    base = 2 * W + R + FF_LANES

    def rotary(t, partner):
        return jnp.concatenate([t[:, :R] * cos_ref[0] + partner * sin_ref[0], t[:, R:]], axis=1)

    pa = jnp.dot(h, wa_ref[:, 0:base], preferred_element_type=F32)
    fk_ref[0] = pa[:, 0:W].astype(BF16)
    dk_ref[0] = rotary(pa[:, W:2 * W], pa[:, 2 * W:2 * W + R]).astype(BF16)
    z = pa[:, 2 * W + R:base] + fb_ref[...]
    lf_ref[0] = (jnp.minimum(z, 0.0) - jnp.log1p(jnp.exp(-jnp.abs(z)))) * LOG2E

    hdot = lambda i: jnp.dot(h, wa_ref[:, base + i * HG_W:base + (i + 1) * HG_W],
                             preferred_element_type=F32)
    q = hdot(0)
    hq_ref[0] = q * _sigmoid(q)
    z = hdot(1)
    lbf = jnp.clip(lb_ref[...], 0.0, 1.0 - 1e-6)
    ez = jnp.exp(-jnp.abs(z))
    inv = 1.0 / (1.0 + ez)
    sig_pos = jnp.where(z >= 0, inv, ez * inv)
    sig_neg = jnp.where(z >= 0, ez * inv, inv)
    hl_ref[0] = jnp.log(jnp.maximum(lbf, LB_FLOOR) + (1.0 - lbf) * sig_pos) * LOG2E
    hk_ref[0] = (1.0 - lbf) * sig_neg
    hv_ref[0] = hdot(2)
    hs_ref[0] = _sigmoid(hdot(3))

    pb = jnp.dot(h, wb_ref[...], preferred_element_type=F32)
    fqt_ref[0] = (pb[:, 0:W] * QK_SCALE).T.astype(BF16)
    dq = rotary(pb[:, 2 * W:3 * W], pb[:, 3 * W:3 * W + R]) * QK_SCALE
    dqt_ref[0] = dq.T.astype(BF16)
    ones = jnp.ones((VT_ROWS - HEAD_DIM, BK), BF16)
    for src, ref in ((pb[:, W:2 * W].T.astype(BF16), fvt_ref),
                     (pb[:, 3 * W + R:4 * W + R].T.astype(BF16), dvt_ref)):
        for j in range(TM_PROJ // BK):
            ref[0, j] = jnp.concatenate(
                [part for h in range(N_HEADS)
                 for part in (src[h * HEAD_DIM:(h + 1) * HEAD_DIM, j * BK:(j + 1) * BK], ones)],
                axis=0)


def _inproj(x, layer, g, wa, wb, cos, sin, fb, lb):
    B, T, _ = x.shape
    W = ATT_W
    nt = T // TM_PROJ
    per_layer = lambda arr: pl.BlockSpec((None,) + arr.shape[1:], lambda b, t: (layer, 0, 0),
                                         pipeline_mode=pl.Buffered(1))
    row = lambda b, t: (b, t, 0)
    col = lambda b, t: (b, 0, t)
    blk4 = lambda b, t: (b, t, 0, 0)
    hg_shape = jax.ShapeDtypeStruct((B, T, HG_W), F32)
    hg_spec = pl.BlockSpec((1, TM_PROJ, HG_W), row)
    out_shape = [
        jax.ShapeDtypeStruct((B, T, W), BF16),
        jax.ShapeDtypeStruct((B, T, FF_LANES), F32),
        jax.ShapeDtypeStruct((B, T, W), BF16),
        hg_shape, hg_shape, hg_shape, hg_shape, hg_shape,
        jax.ShapeDtypeStruct((B, W, T), BF16),
        jax.ShapeDtypeStruct((B, T // BK, N_HEADS * VT_ROWS, BK), BF16),
        jax.ShapeDtypeStruct((B, W, T), BF16),
        jax.ShapeDtypeStruct((B, T // BK, N_HEADS * VT_ROWS, BK), BF16),
    ]
    out_specs = [
        pl.BlockSpec((1, TM_PROJ, W), row),
        pl.BlockSpec((1, TM_PROJ, FF_LANES), row),
        pl.BlockSpec((1, TM_PROJ, W), row),
        hg_spec, hg_spec, hg_spec, hg_spec, hg_spec,
        pl.BlockSpec((1, W, TM_PROJ), col),
        pl.BlockSpec((1, TM_PROJ // BK, N_HEADS * VT_ROWS, BK), blk4),
        pl.BlockSpec((1, W, TM_PROJ), col),
        pl.BlockSpec((1, TM_PROJ // BK, N_HEADS * VT_ROWS, BK), blk4),
    ]
    return pl.pallas_call(
        _inproj_kernel,
        grid=(B, nt),
        in_specs=[
            pl.BlockSpec((1, TM_PROJ, D_MODEL), row),
            per_layer(g), per_layer(wa), per_layer(wb),
            pl.BlockSpec((1, TM_PROJ, ROT_LANES), row),
            pl.BlockSpec((1, TM_PROJ, ROT_LANES), row),
            per_layer(fb), per_layer(lb),
        ],
        out_specs=out_specs,
        out_shape=out_shape,
        compiler_params=_params(("parallel", "parallel")),
        name="inproj",
    )(x, g, wa, wb, cos, sin, fb, lb)


def _attn_pipeline(fox, qi, qt_ref, k_ref, vt_ref, aux_ref, sel_ref, o_ref,
                   qm_ref, acc_ref, m_ref, st_slots, ka_ref):
    T = k_ref.shape[1]
    R = BQ // BK
    assert R == 2
    qt = qt_ref[0]

    if fox:
        @pl.when(qi == 0)
        def _():
            tril3 = _tril3(CS_BLK)
            blocks = [slice(i * CS_BLK, (i + 1) * CS_BLK) for i in range(T // CS_BLK)]
            local = [_cumsum_rows(tril3, _split3(aux_ref[0, rows, :])) for rows in blocks]
            carry = jnp.zeros((1, FF_LANES), F32)
            for rows, loc in zip(blocks, local):
                blk = loc + carry
                carry = blk[CS_BLK - 1:CS_BLK, :]
                cat = jnp.concatenate([k_ref[0, rows, :], *_split3(blk)], axis=1)
                ka_ref[rows, :] = jnp.dot(cat, sel_ref[...],
                                          preferred_element_type=F32).astype(BF16)

        pad_row = lax.broadcasted_iota(jnp.int32, (HEAD_DIM, BQ), 0)
        minus_ones = jnp.where(pad_row < N_SPLIT, -1.0, 0.0).astype(BF16)
        for h in range(N_HEADS):
            qm_ref[h] = jnp.concatenate([qt[h * HEAD_DIM:(h + 1) * HEAD_DIM], minus_ones], axis=0)
    else:
        r = lax.broadcasted_iota(jnp.int32, qt.shape, 0)
        n_rot = N_HEADS * ROPE_DIM
        head_of_row = jnp.where(r < n_rot, r // ROPE_DIM, (r - n_rot) // REST_DIM)
        for h in range(N_HEADS):
            qm_ref[h] = jnp.where(head_of_row == h, qt, jnp.zeros_like(qt))
    acc_ref[...] = jnp.zeros_like(acc_ref)
    m_ref[...] = jnp.full(m_ref.shape, NEG_BIG, F32)
    krow = lax.broadcasted_iota(jnp.int32, (BK, BQ), 0)
    qcol = lax.broadcasted_iota(jnp.int32, (BK, BQ), 1)
    tri = (lax.broadcasted_iota(jnp.int32, (BK, BK), 0)
           <= lax.broadcasted_iota(jnp.int32, (BK, BK), 1))
    upper = slice(BK, BQ)

    def fetch(j, last=False):
        start = pl.multiple_of(j * BK, BK)
        if fox:
            kbs = [ka_ref[pl.ds(start, BK), 2 * h * HEAD_DIM:2 * (h + 1) * HEAD_DIM]
                   for h in range(N_HEADS)]
            aux = None
        else:
            kbs = [k_ref[0, pl.ds(start, BK), :]] * N_HEADS
            halves = (1,) if last else range(R)
            aux = jnp.concatenate([aux_ref[R * qi + half - j] for half in halves], axis=1)
        vbs = [vt_ref[0, j, h * VT_ROWS:(h + 1) * VT_ROWS, :] for h in range(N_HEADS)]
        return kbs, vbs, aux

    def head_stage(h, blk, slot, nxt=None, nxt_last=False, diag=False, last=False):
        _, vbs, aux = blk
        cols = upper if last else slice(0, BQ)
        nxt_cols = upper if nxt_last else slice(0, BQ)
        if nxt is not None:
            st_slots[1 - slot][h, :, nxt_cols] = jnp.dot(
                nxt[0][h], qm_ref[h, :, nxt_cols], preferred_element_type=F32)
        st = st_slots[slot][h, :, cols]
        if not fox:
            st = st + aux
        elif last:
            st = jnp.where(tri, st, NEG_BIG)
        elif diag:
            st = jnp.where(krow <= qcol, st, NEG_BIG)
        m_old = m_ref[h, :, cols]
        m_new = jnp.maximum(m_old, jnp.max(st, axis=0, keepdims=True))
        m_ref[h, :, cols] = m_new
        p = jnp.exp2(st - m_new)
        alpha = jnp.exp2(m_old - m_new)
        rows = slice(h * VT_ROWS, (h + 1) * VT_ROWS)
        acc_ref[rows, cols] = alpha * acc_ref[rows, cols] + jnp.dot(
            vbs[h], p.astype(BF16), preferred_element_type=F32)

    def first_scores(h, blk):
        st_slots[0][h] = jnp.dot(blk[0][h], qm_ref[h], preferred_element_type=F32)

    def finish():
        outs = []
        for h in range(N_HEADS):
            num = acc_ref[h * VT_ROWS:h * VT_ROWS + HEAD_DIM, :]
            den = acc_ref[h * VT_ROWS + HEAD_DIM:h * VT_ROWS + HEAD_DIM + 1, :]
            outs.append(num / den)
        o_ref[0] = jnp.concatenate(outs, axis=0).T.astype(BF16)

    return fetch, head_stage, first_scores, finish


def _attn_kernel(fqt_ref, fk_ref, fvt_ref, lf_ref, sel_ref, dqt_ref, dk_ref, dvt_ref, tab_ref,
                 of_ref, od_ref,
                 fqm, facc, fm, fst0, fst1, fka, dqm, dacc, dm, dst0, dst1):
    qi = pl.program_id(1)
    R = BQ // BK
    groups = (
        _attn_pipeline(True, qi, fqt_ref, fk_ref, fvt_ref, lf_ref, sel_ref, of_ref,
                       fqm, facc, fm, (fst0, fst1), fka),
        _attn_pipeline(False, qi, dqt_ref, dk_ref, dvt_ref, tab_ref, None, od_ref,
                       dqm, dacc, dm, (dst0, dst1), None),
    )

    def stage(blks, slot, nxts=None, **flags):
        for h in range(N_HEADS):
            for g, (_, head_stage, _, _) in enumerate(groups):
                head_stage(h, blks[g], slot, nxt=None if nxts is None else nxts[g], **flags)

    def fetch_all(j, **kw):
        return [fetch(j, **kw) for fetch, _, _, _ in groups]

    def pair(i, _):
        b0, b1, b2 = fetch_all(2 * i), fetch_all(2 * i + 1), fetch_all(2 * i + 2)
        stage(b0, 0, b1)
        stage(b1, 1, b2)
        return 0

    first = fetch_all(0)
    for h in range(N_HEADS):
        for g, (_, _, first_scores, _) in enumerate(groups):
            first_scores(h, first[g])
    lax.fori_loop(0, qi, pair, 0)
    b0, b1 = fetch_all(R * qi), fetch_all(R * qi + 1, last=True)
    stage(b0, 0, b1, nxt_last=True, diag=True)
    stage(b1, 1, last=True)
    for _, _, _, finish in groups:
        finish()


def _fox_slot_matrix():
    sel = np.zeros((ATT_W + N_SPLIT * FF_LANES, N_HEADS * 2 * HEAD_DIM), np.float32)
    for h in range(N_HEADS):
        for d in range(HEAD_DIM):
            sel[h * HEAD_DIM + d, 2 * h * HEAD_DIM + d] = 1.0
        for piece in range(N_SPLIT):
            sel[ATT_W + piece * FF_LANES + h, (2 * h + 1) * HEAD_DIM + piece] = 1.0
    return jnp.asarray(sel, BF16)


def _attention(fqt, fk, fvt, lf, dqt, dk, dvt, tab):
    B, T, W = fk.shape
    sel = _fox_slot_matrix()
    q_spec = pl.BlockSpec((1, W, BQ), lambda b, q: (b, 0, q))
    k_spec = pl.BlockSpec((1, T, W), lambda b, q: (b, 0, 0))
    v_spec = pl.BlockSpec((1, T // BK, N_HEADS * VT_ROWS, BK), lambda b, q: (b, 0, 0, 0))
    o_spec = pl.BlockSpec((1, BQ, W), lambda b, q: (b, q, 0))
    o_shape = jax.ShapeDtypeStruct((B, T, W), BF16)

    def group_scratch(q_rows):
        return [pltpu.VMEM((N_HEADS, q_rows, BQ), BF16),
                pltpu.VMEM((N_HEADS * VT_ROWS, BQ), F32),
                pltpu.VMEM((N_HEADS, 1, BQ), F32),
                pltpu.VMEM((N_HEADS, BK, BQ), F32), pltpu.VMEM((N_HEADS, BK, BQ), F32)]

    return pl.pallas_call(
        _attn_kernel,
        grid=(B, T // BQ),
        in_specs=[q_spec, k_spec, v_spec,
                  pl.BlockSpec((1, T, FF_LANES), lambda b, q: (b, 0, 0)),
                  pl.BlockSpec(sel.shape, lambda b, q: (0, 0)),
                  q_spec, k_spec, v_spec,
                  pl.BlockSpec(tab.shape, lambda b, q: (0, 0, 0))],
        out_specs=[o_spec, o_spec],
        out_shape=[o_shape, o_shape],
        scratch_shapes=(group_scratch(2 * HEAD_DIM)
                        + [pltpu.VMEM((T, N_HEADS * 2 * HEAD_DIM), BF16)]
                        + group_scratch(W)),
        compiler_params=_params(("parallel", "arbitrary")),
        name="attention",
    )(fqt, fk, fvt, lf, sel, dqt, dk, dvt, tab)


def _dilated_bias_table():
    nd = max(w for w, _ in DILATED_PATTERNS) // BK
    d = (np.arange(nd)[:, None, None] * BK + np.arange(BK)[None, None, :]
         - np.arange(BK)[None, :, None])
    mult = np.zeros(d.shape, np.float64)
    for window, dil in DILATED_PATTERNS:
        mult += (d >= 0) & (d <= window) & (d % dil == 0)
    tab = np.where(mult > 0, np.log2(np.maximum(mult, 1.0)), NEG_BIG)
    return jnp.asarray(tab, F32)


def _hgrn_kernel(q_ref, lf_ref, k_ref, v_ref, s_ref, nw_ref, o_ref,
                 st_ref, b_scr, k_scr, a_scr, g_scr, bn_scr, ta_scr, ti_scr, tv_scr, tg_scr):
    C, S, E = HG_CHUNK, HG_SUB, HG_E
    NS = C // S
    n_chunks = HG_TB // C
    heads = range(HG_HEADS)
    lanes = [slice(hd * E, (hd + 1) * E) for hd in heads]

    @pl.when(pl.program_id(1) == 0)
    def _():
        st_ref[...] = jnp.zeros_like(st_ref)

    r = lax.broadcasted_iota(jnp.int32, (C, C), 0)
    c = lax.broadcasted_iota(jnp.int32, (C, C), 1)
    tril3 = _tril3(C)
    diag01 = jnp.where(((c // S) == (r // S)) & (c <= r), 1.0, 0.0)
    rr = lax.broadcasted_iota(jnp.int32, (S * E, C), 0)
    rc = lax.broadcasted_iota(jnp.int32, (S * E, C), 1)
    red = jnp.where(rr // E == rc % S, 1.0, 0.0).astype(BF16)
    not_first01 = jnp.where(lax.broadcasted_iota(jnp.int32, (NS, E), 0) == 0, 0.0, 1.0)
    zeros_sub = jnp.zeros((S, E), F32)

    def rows_bcast(ref, hd, idx):
        return jnp.concatenate(
            [jnp.broadcast_to(ref[hd, idx(i):idx(i) + 1, :], (S, E)) for i in range(NS)], axis=0)

    def cumsum(ci):
        r0 = pl.multiple_of(ci * C, C)
        return [_cumsum_rows(tril3, _split3(lf_ref[0, pl.ds(r0, C), lanes[hd]])) for hd in heads]

    def out_dots():
        return [ti_scr[hd] + jnp.dot(ta_scr[hd], tv_scr[hd], preferred_element_type=F32)
                for hd in heads]

    def out_finish(ci, os_, gates, hd):
        r0 = pl.multiple_of(ci * C, C)
        o = os_[hd]
        o = o * lax.rsqrt(jnp.mean(o * o, axis=-1, keepdims=True) + EPS)
        o_ref[0, pl.ds(r0, C), lanes[hd]] = (o * gates[hd]).astype(BF16)

    def scores(ci, bs_, after_head):
        r0 = pl.multiple_of(ci * C, C)
        for hd in heads:
            b = bs_[hd]
            q = q_ref[0, pl.ds(r0, C), lanes[hd]]
            k = k_ref[0, pl.ds(r0, C), lanes[hd]]
            v = v_ref[0, pl.ds(r0, C), lanes[hd]]
            b_last = b[C - 1:C, :]
            st = st_ref[hd]
            qe = (q * jnp.exp2(b)).astype(BF16)
            ti_scr[hd] = lax.dot_general(qe, st.astype(BF16), (((1,), (1,)), ((), ())),
                                         preferred_element_type=F32)
            kdec = (k * jnp.exp2(b_last - b)).astype(BF16)
            st_ref[hd] = st * jnp.exp2(b_last) + jnp.dot(
                v.T.astype(BF16), kdec, preferred_element_type=F32)
            tv_scr[hd] = v.astype(BF16)
            tg_scr[hd] = s_ref[0, pl.ds(r0, C), lanes[hd]] * nw_ref[:, lanes[hd]]
            b_scr[hd] = b
            k_scr[hd] = k
            b_end = b_scr[hd, pl.ds(S - 1, NS, stride=S), :]
            b_prev = pltpu.roll(b_end, 1, axis=0) * not_first01
            a_scr[hd, 0:NS] = b_end
            a_scr[hd, NS:2 * NS] = b_prev
            q_t = q * jnp.exp2(b - rows_bcast(a_scr, hd, lambda i: NS + i))
            k_h = k * jnp.exp2(rows_bcast(a_scr, hd, lambda i: i) - b)
            for j in range(NS - 1):
                g_scr[hd, j] = jnp.exp2(jnp.minimum(b_prev - b_end[j:j + 1, :], 0.0))
            lhs_rows, rhs_rows = [], []
            for i in range(NS):
                qi_t = q_t[i * S:(i + 1) * S, :]
                lhs_rows.append(jnp.concatenate(
                    [qi_t * g_scr[hd, j, i:i + 1, :] if i > j else zeros_sub
                     for j in range(NS - 1)], axis=1))
                rhs_rows.append(jnp.concatenate(
                    [k_h[i * S:(i + 1) * S, :] if i == j else zeros_sub
                     for j in range(NS - 1)], axis=1))
            a_off = lax.dot_general(jnp.concatenate(lhs_rows, axis=0).astype(BF16),
                                    jnp.concatenate(rhs_rows, axis=0).astype(BF16),
                                    (((1,), (1,)), ((), ())),
                                    preferred_element_type=F32)
            w_parts = []
            for s in range(S):
                bs = rows_bcast(b_scr, hd, lambda i: i * S + s)
                ks = rows_bcast(k_scr, hd, lambda i: i * S + s)
                w_parts.append((jnp.exp2(jnp.minimum(b - bs, 0.0)) * (q * ks)).astype(BF16))
            a_diag = jnp.dot(jnp.concatenate(w_parts, axis=1), red,
                             preferred_element_type=F32)
            ta_scr[hd] = (a_off + a_diag * diag01).astype(BF16)
            after_head(hd)

    def step(ci, _):
        bs_ = [bn_scr[hd] for hd in heads]
        gates = [tg_scr[hd] for hd in heads]
        nxt = cumsum(jnp.minimum(ci + 1, n_chunks - 1))
        os_ = out_dots()
        scores(ci, bs_, functools.partial(out_finish, jnp.maximum(ci - 1, 0), os_, gates))
        for hd in heads:
            bn_scr[hd] = nxt[hd]
        return 0

    for ref in (ta_scr, ti_scr, tv_scr, tg_scr):
        ref[...] = jnp.zeros_like(ref)
    first = cumsum(0)
    for hd in heads:
        bn_scr[hd] = first[hd]
    lax.fori_loop(0, n_chunks, step, 0)
    last_os, last_gates = out_dots(), [tg_scr[hd] for hd in heads]
    for hd in heads:
        out_finish(n_chunks - 1, last_os, last_gates, hd)


def _hgrn(hq, hl, hk, hv, hs, layer, nw):
    B, T, W = hq.shape
    row = lambda b, t: (b, t, 0)
    blk = pl.BlockSpec((1, HG_TB, W), row)
    C, NS = HG_CHUNK, HG_CHUNK // HG_SUB
    per_head = lambda *shape, dtype=F32: pltpu.VMEM((HG_HEADS,) + shape, dtype)
    return pl.pallas_call(
        _hgrn_kernel,
        grid=(B, T // HG_TB),
        in_specs=[blk, blk, blk, blk, blk,
                  pl.BlockSpec((None, 1, W), lambda b, t: (layer, 0, 0))],
        out_specs=pl.BlockSpec((1, HG_TB, W), row),
        out_shape=jax.ShapeDtypeStruct((B, T, W), BF16),
        scratch_shapes=[
            per_head(HG_V, HG_E),
            per_head(C, HG_E),
            per_head(C, HG_E),
            per_head(2 * NS, HG_E),
            per_head(NS - 1, NS, HG_E),
            per_head(C, HG_E),
            per_head(C, C, dtype=BF16),
            per_head(C, HG_V),
            per_head(C, HG_V, dtype=BF16),
            per_head(C, HG_V),
        ],
        compiler_params=_params(("parallel", "arbitrary")),
        name="hgrn2",
    )(hq, hl, hk, hv, hs, nw)


W_IN_SIZES = (ATT_W, ATT_W, ATT_W, N_HEADS, ATT_W, ATT_W, ATT_W, HG_W, HG_W, HG_W, HG_W)
W_IN_GROUP = 2 * LANES
COPY_LO, COPY_HI, FORGET, PERM_LO, PERM_HI, ROTARY = range(6)


def _relayout_plan():
    fq, fk, fv, ff, dq, dk, dv, hq, hf, hi, hg = (int(e) for e in np.cumsum((0,) + W_IN_SIZES)[:-1])
    copy = lambda start, width: [(start + o, kind) for o in range(0, width, W_IN_GROUP)
                                 for kind in (COPY_LO, COPY_HI)]
    dilated = lambda start: [(start, PERM_LO), (start, PERM_HI), (start, ROTARY)]
    wa = copy(fk, ATT_W) + dilated(dk) + [(ff, FORGET)] + copy(hq, 4 * HG_W)
    wb = copy(fq, ATT_W) + copy(fv, ATT_W) + dilated(dq) + copy(dv, ATT_W)
    return wa, wb


def _relayout_kernel(start_ref, kind_ref, w_ref, o_ref):
    del start_ref
    kind = kind_ref[pl.program_id(0)]
    half = ROPE_DIM // 2
    row = lax.broadcasted_iota(jnp.int32, (LANES, D_MODEL), 0)
    n_rot = N_HEADS * ROPE_DIM

    def head(x, h, lo, hi):
        return x[h * HEAD_DIM + lo:h * HEAD_DIM + hi]

    def rest(x):
        return jnp.concatenate([head(x, h, ROPE_DIM, HEAD_DIM) for h in range(N_HEADS)], axis=0)

    def perm_lo(x):
        return jnp.concatenate([head(x, h, 0, ROPE_DIM) for h in range(N_HEADS)]
                               + [rest(x)[:LANES - n_rot]], axis=0)

    def rotary(x):
        return jnp.concatenate(
            [part for h in range(N_HEADS)
             for part in (-head(x, h, half, ROPE_DIM), head(x, h, 0, half))]
            + [jnp.zeros((LANES - n_rot, D_MODEL), F32)], axis=0)

    variants = {COPY_LO: lambda x: x[:LANES], COPY_HI: lambda x: x[LANES:],
                FORGET: lambda x: jnp.where(row < N_HEADS, x[:LANES], 0.0),
                PERM_LO: perm_lo, PERM_HI: lambda x: rest(x)[LANES - n_rot:], ROTARY: rotary}
    for k, variant in variants.items():
        @pl.when(kind == k)
        def _(variant=variant):
            for layer in range(o_ref.shape[0]):
                x = w_ref[:, layer, :]
                o_ref[layer] = variant(x).T.astype(BF16)


def _relayout_w_in(w_in):
    depth, d_model, _ = w_in.shape
    w_t = w_in.transpose(2, 0, 1)
    outs = []
    for plan in _relayout_plan():
        starts = jnp.asarray([s for s, _ in plan], jnp.int32)
        kinds = jnp.asarray([k for _, k in plan], jnp.int32)
        outs.append(pl.pallas_call(
            _relayout_kernel,
            grid_spec=pltpu.PrefetchScalarGridSpec(
                num_scalar_prefetch=2,
                grid=(len(plan),),
                in_specs=[pl.BlockSpec(
                    (pl.Element(W_IN_GROUP), pl.Element(depth), pl.Element(d_model)),
                    lambda g, starts, kinds: (starts[g], 0, 0))],
                out_specs=pl.BlockSpec((depth, d_model, LANES), lambda g, starts, kinds: (0, 0, g)),
            ),
            out_shape=jax.ShapeDtypeStruct((depth, d_model, len(plan) * LANES), BF16),
            compiler_params=_params(("arbitrary",)),
            name="w_in_relayout",
        )(starts, kinds, w_t))
    return outs


def _rope_tables(positions):
    half = ROPE_DIM // 2
    freqs = ROPE_THETA ** (-jnp.arange(0, ROPE_DIM, 2, dtype=F32) / ROPE_DIM)
    ang = positions.astype(F32)[:, :, None] * freqs
    lane = np.arange(ROT_LANES)
    is_rot = lane < N_HEADS * ROPE_DIM
    place = ((lane[None, :] % half == np.arange(half)[:, None]) & is_rot[None, :]).astype(np.float32)
    expand = lambda t: jnp.einsum("bth,hl->btl", t, place, precision=lax.Precision.HIGHEST)
    cos = expand(jnp.cos(ang)) + jnp.asarray(~is_rot, F32)
    return cos, expand(jnp.sin(ang))


def kernel(x, positions, ffn1_norm, ffn1_w_gate, ffn1_w_up, ffn1_w_down, mix_norm, w_in,
           fox_forget_bias, hgrn_lower_bounds, hgrn_out_norm, w_out, ffn2_norm,
           ffn2_w_gate, ffn2_w_up, ffn2_w_down, final_norm):
    B, T, D = x.shape
    depth = w_in.shape[0]
    assert D == D_MODEL and T % max(TM_PROJ, HG_TB, BQ) == 0 and (B * T) % TM_FFN == 0
    assert BQ % BK == 0 and TM_PROJ % BK == 0

    sm = jax.nn.softmax(hgrn_lower_bounds.astype(F32), axis=0)
    lbs = (jnp.cumsum(sm, axis=0) - sm[0:1]).reshape(depth, 1, HG_W)
    wa, wb = _relayout_w_in(w_in)
    fb = jnp.pad(fox_forget_bias.astype(F32), ((0, 0), (0, FF_LANES - N_HEADS)))
    fb = fb.reshape(depth, 1, FF_LANES)
    stack_row = lambda p: p.astype(F32).reshape(depth, 1, p.shape[-1])
    n1, nm, n2, nh = (stack_row(p) for p in (ffn1_norm, mix_norm, ffn2_norm, hgrn_out_norm))
    w1 = (ffn1_w_gate, ffn1_w_up, ffn1_w_down)
    w2 = (ffn2_w_gate, ffn2_w_up, ffn2_w_down)
    wo = w_out
    fnorm = final_norm.astype(F32).reshape(1, D)

    cos, sin = _rope_tables(positions)
    dil_tab = _dilated_bias_table()

    xf = x.reshape(B * T, D)
    for i in range(depth):
        xf = _ffn(xf, i, n1, *w1)
        (fk_a, lf_a, dk_a, hq_a, hl_a, hk_a, hv_a, hs_a, fqt_a, fvt_a, dqt_a, dvt_a) = _inproj(
            xf.reshape(B, T, D), i, nm, wa, wb, cos, sin, fb, lbs)
        oa, ob = _attention(fqt_a, fk_a, fvt_a, lf_a, dqt_a, dk_a, dvt_a, dil_tab)
        oc = _hgrn(hq_a, hl_a, hk_a, hv_a, hs_a, i, nh)
        xf = _mix_ffn(xf, oa.reshape(B * T, ATT_W), ob.reshape(B * T, ATT_W),
                      oc.reshape(B * T, HG_W), i, wo, n2, *w2, fnorm, i == depth - 1)
    return xf.reshape(B, T, D)
```

```python
import functools
import math

import numpy as np
import jax
import jax.numpy as jnp
from jax import lax
from jax.experimental import pallas as pl
from jax.experimental.pallas import tpu as pltpu

F32 = jnp.float32
BF16 = jnp.bfloat16

D_MODEL = 1024
HEAD_DIM = 64
N_HEADS = 4
ATT_W = N_HEADS * HEAD_DIM
HG_HEADS = 4
HG_E = 128
HG_V = 128
HG_W = HG_HEADS * HG_V
ROPE_DIM = HEAD_DIM // 4
ROPE_THETA = 500000.0
D_FF = ((8 * D_MODEL // 3 + 127) // 128) * 128
EPS = 1e-6
NEG_BIG = -1e30
LB_FLOOR = 1e-30
DILATED_PATTERNS = ((128, 1), (512, 4), (2048, 16))
FF_LANES = 128
VT_ROWS = HEAD_DIM + 16
LOG2E = 1.0 / math.log(2.0)
QK_SCALE = HEAD_DIM ** -0.5 * LOG2E
N_SPLIT = 3
ROT_LANES = 128
REST_DIM = HEAD_DIM - ROPE_DIM

V7X_VMEM_BYTES = 64 * 1024 * 1024
VMEM_LIMIT = V7X_VMEM_BYTES - 8 * 1024 * 1024
SUBLANES = 8
LANES = 128

TM_FFN = 512
FF_CHUNK = 512
TM_PROJ = 1024
BQ = 512
BK = 256
HG_CHUNK = 64
HG_SUB = SUBLANES
HG_TB = 1024
CS_BLK = 256


def _rms(x, g):
    ms = jnp.mean(x * x, axis=-1, keepdims=True)
    return x * lax.rsqrt(ms + EPS) * g


def _sigmoid(x):
    return 1.0 / (1.0 + jnp.exp(-x))


def _split3(x):
    hi = x.astype(BF16)
    rest = x - hi.astype(F32)
    mid = rest.astype(BF16)
    lo = (rest - mid.astype(F32)).astype(BF16)
    return hi, mid, lo


def _tril3(n):
    width = -(-N_SPLIT * n // LANES) * LANES
    r = lax.broadcasted_iota(jnp.int32, (n, width), 0)
    c = lax.broadcasted_iota(jnp.int32, (n, width), 1)
    return jnp.where((c % n <= r) & (c < N_SPLIT * n), 1.0, 0.0).astype(BF16)


def _cumsum_rows(tril3, pieces):
    pieces = list(pieces)
    rows, cols = pieces[0].shape
    pad = tril3.shape[1] - N_SPLIT * rows
    if pad:
        pieces.append(jnp.zeros((pad, cols), BF16))
    return jnp.dot(tril3, jnp.concatenate(pieces, axis=0), preferred_element_type=F32)


def _params(sem):
    return pltpu.CompilerParams(dimension_semantics=sem, vmem_limit_bytes=VMEM_LIMIT)


def _swiglu_residual(x, g_ref, wg_ref, wu_ref, wd_ref):
    h = _rms(x, g_ref[...]).astype(BF16)
    acc = jnp.zeros_like(x)
    for c in range(0, D_FF, FF_CHUNK):
        w = min(FF_CHUNK, D_FF - c)
        g = jnp.dot(h, wg_ref[:, c:c + w].astype(BF16), preferred_element_type=F32)
        u = jnp.dot(h, wu_ref[:, c:c + w].astype(BF16), preferred_element_type=F32)
        a = (g * _sigmoid(g) * u).astype(BF16)
        acc = acc + jnp.dot(a, wd_ref[c:c + w, :].astype(BF16), preferred_element_type=F32)
    return x + 0.5 * acc


def _ffn_kernel(x_ref, g_ref, wg_ref, wu_ref, wd_ref, o_ref):
    o_ref[...] = _swiglu_residual(x_ref[...], g_ref, wg_ref, wu_ref, wd_ref)


def _mix_ffn_kernel(final, x_ref, oa_ref, ob_ref, oc_ref, wo_ref, g_ref, wg_ref, wu_ref, wd_ref,
                    fg_ref, o_ref):
    W = ATT_W
    wo = lambda lo, hi: wo_ref[lo:hi, :].astype(BF16)
    x = x_ref[...] + jnp.dot(oa_ref[...], wo(0, W), preferred_element_type=F32)
    x = x + jnp.dot(ob_ref[...], wo(W, 2 * W), preferred_element_type=F32)
    x = x + jnp.dot(oc_ref[...], wo(2 * W, D_MODEL), preferred_element_type=F32)
    y = _swiglu_residual(x, g_ref, wg_ref, wu_ref, wd_ref)
    if final:
        y = _rms(y, fg_ref[...])
    o_ref[...] = y


def _layer_spec(arr, layer):
    return pl.BlockSpec((None,) + arr.shape[1:], lambda i: (layer, 0, 0),
                        pipeline_mode=pl.Buffered(1))


def _ffn(x2d, layer, g, wg, wu, wd):
    n = x2d.shape[0]
    row = lambda i: (i, 0)
    return pl.pallas_call(
        _ffn_kernel,
        grid=(n // TM_FFN,),
        in_specs=[pl.BlockSpec((TM_FFN, D_MODEL), row), _layer_spec(g, layer),
                  _layer_spec(wg, layer), _layer_spec(wu, layer), _layer_spec(wd, layer)],
        out_specs=pl.BlockSpec((TM_FFN, D_MODEL), row),
        out_shape=jax.ShapeDtypeStruct((n, D_MODEL), F32),
        compiler_params=_params(("parallel",)),
        name="ffn",
    )(x2d, g, wg, wu, wd)


def _mix_ffn(x2d, oa, ob, oc, layer, wo, g, wg, wu, wd, fg, final):
    n = x2d.shape[0]
    row = lambda i: (i, 0)
    return pl.pallas_call(
        functools.partial(_mix_ffn_kernel, final),
        grid=(n // TM_FFN,),
        in_specs=[pl.BlockSpec((TM_FFN, D_MODEL), row),
                  pl.BlockSpec((TM_FFN, ATT_W), row),
                  pl.BlockSpec((TM_FFN, ATT_W), row),
                  pl.BlockSpec((TM_FFN, HG_W), row),
                  _layer_spec(wo, layer), _layer_spec(g, layer),
                  _layer_spec(wg, layer), _layer_spec(wu, layer), _layer_spec(wd, layer),
                  pl.BlockSpec((1, D_MODEL), lambda i: (0, 0))],
        out_specs=pl.BlockSpec((TM_FFN, D_MODEL), row),
        out_shape=jax.ShapeDtypeStruct((n, D_MODEL), F32),
        compiler_params=_params(("parallel",)),
        name="mix_ffn_final" if final else "mix_ffn",
    )(x2d, oa, ob, oc, wo, g, wg, wu, wd, fg)


def _inproj_kernel(x_ref, g_ref, wa_ref, wb_ref, cos_ref, sin_ref, fb_ref, lb_ref,
                   fk_ref, lf_ref, dk_ref, hq_ref, hl_ref, hk_ref, hv_ref, hs_ref,
                   fqt_ref, fvt_ref, dqt_ref, dvt_ref):
    h = _rms(x_ref[0], g_ref[...]).astype(BF16)
    W = ATT_W
    R = ROT_LANES
    base = 2 * W + R + FF_LANES

    def rotary(t, partner):
        return jnp.concatenate([t[:, :R] * cos_ref[0] + partner * sin_ref[0], t[:, R:]], axis=1)

    pa = jnp.dot(h, wa_ref[:, 0:base], preferred_element_type=F32)
    fk_ref[0] = pa[:, 0:W].astype(BF16)
    dk_ref[0] = rotary(pa[:, W:2 * W], pa[:, 2 * W:2 * W + R]).astype(BF16)
    z = pa[:, 2 * W + R:base] + fb_ref[...]
    lf_ref[0] = (jnp.minimum(z, 0.0) - jnp.log1p(jnp.exp(-jnp.abs(z)))) * LOG2E

    hdot = lambda i: jnp.dot(h, wa_ref[:, base + i * HG_W:base + (i + 1) * HG_W],
                             preferred_element_type=F32)
    q = hdot(0)
    hq_ref[0] = q * _sigmoid(q)
    z = hdot(1)
    lbf = jnp.clip(lb_ref[...], 0.0, 1.0 - 1e-6)
    ez = jnp.exp(-jnp.abs(z))
    inv = 1.0 / (1.0 + ez)
    sig_pos = jnp.where(z >= 0, inv, ez * inv)
    sig_neg = jnp.where(z >= 0, ez * inv, inv)
    hl_ref[0] = jnp.log(jnp.maximum(lbf, LB_FLOOR) + (1.0 - lbf) * sig_pos) * LOG2E
    hk_ref[0] = (1.0 - lbf) * sig_neg
    hv_ref[0] = hdot(2)
    hs_ref[0] = _sigmoid(hdot(3))

    pb = jnp.dot(h, wb_ref[...], preferred_element_type=F32)
    fqt_ref[0] = (pb[:, 0:W] * QK_SCALE).T.astype(BF16)
    dq = rotary(pb[:, 2 * W:3 * W], pb[:, 3 * W:3 * W + R]) * QK_SCALE
    dqt_ref[0] = dq.T.astype(BF16)
    ones = jnp.ones((VT_ROWS - HEAD_DIM, BK), BF16)
    for src, ref in ((pb[:, W:2 * W].T.astype(BF16), fvt_ref),
                     (pb[:, 3 * W + R:4 * W + R].T.astype(BF16), dvt_ref)):
        for j in range(TM_PROJ // BK):
            ref[0, j] = jnp.concatenate(
                [part for h in range(N_HEADS)
                 for part in (src[h * HEAD_DIM:(h + 1) * HEAD_DIM, j * BK:(j + 1) * BK], ones)],
                axis=0)


def _inproj(x, layer, g, wa, wb, cos, sin, fb, lb):
    B, T, _ = x.shape
    W = ATT_W
    nt = T // TM_PROJ
    per_layer = lambda arr: pl.BlockSpec((None,) + arr.shape[1:], lambda b, t: (layer, 0, 0),
                                         pipeline_mode=pl.Buffered(1))
    row = lambda b, t: (b, t, 0)
    col = lambda b, t: (b, 0, t)
    blk4 = lambda b, t: (b, t, 0, 0)
    hg_shape = jax.ShapeDtypeStruct((B, T, HG_W), F32)
    hg_spec = pl.BlockSpec((1, TM_PROJ, HG_W), row)
    out_shape = [
        jax.ShapeDtypeStruct((B, T, W), BF16),
        jax.ShapeDtypeStruct((B, T, FF_LANES), F32),
        jax.ShapeDtypeStruct((B, T, W), BF16),
        hg_shape, hg_shape, hg_shape, hg_shape, hg_shape,
        jax.ShapeDtypeStruct((B, W, T), BF16),
        jax.ShapeDtypeStruct((B, T // BK, N_HEADS * VT_ROWS, BK), BF16),
        jax.ShapeDtypeStruct((B, W, T), BF16),
        jax.ShapeDtypeStruct((B, T // BK, N_HEADS * VT_ROWS, BK), BF16),
    ]
    out_specs = [
        pl.BlockSpec((1, TM_PROJ, W), row),
        pl.BlockSpec((1, TM_PROJ, FF_LANES), row),
        pl.BlockSpec((1, TM_PROJ, W), row),
        hg_spec, hg_spec, hg_spec, hg_spec, hg_spec,
        pl.BlockSpec((1, W, TM_PROJ), col),
        pl.BlockSpec((1, TM_PROJ // BK, N_HEADS * VT_ROWS, BK), blk4),
        pl.BlockSpec((1, W, TM_PROJ), col),
        pl.BlockSpec((1, TM_PROJ // BK, N_HEADS * VT_ROWS, BK), blk4),
    ]
    return pl.pallas_call(
        _inproj_kernel,
        grid=(B, nt),
        in_specs=[
            pl.BlockSpec((1, TM_PROJ, D_MODEL), row),
            per_layer(g), per_layer(wa), per_layer(wb),
            pl.BlockSpec((1, TM_PROJ, ROT_LANES), row),
            pl.BlockSpec((1, TM_PROJ, ROT_LANES), row),
            per_layer(fb), per_layer(lb),
        ],
        out_specs=out_specs,
        out_shape=out_shape,
        compiler_params=_params(("parallel", "parallel")),
        name="inproj",
    )(x, g, wa, wb, cos, sin, fb, lb)


def _attn_pipeline(fox, qi, qt_ref, k_ref, vt_ref, aux_ref, sel_ref, o_ref,
                   qm_ref, acc_ref, m_ref, st_slots, ka_ref):
    T = k_ref.shape[1]
    R = BQ // BK
    assert R == 2
    qt = qt_ref[0]

    if fox:
        @pl.when(qi == 0)
        def _():
            tril3 = _tril3(CS_BLK)
            blocks = [slice(i * CS_BLK, (i + 1) * CS_BLK) for i in range(T // CS_BLK)]
            local = [_cumsum_rows(tril3, _split3(aux_ref[0, rows, :])) for rows in blocks]
            carry = jnp.zeros((1, FF_LANES), F32)
            for rows, loc in zip(blocks, local):
                blk = loc + carry
                carry = blk[CS_BLK - 1:CS_BLK, :]
                cat = jnp.concatenate([k_ref[0, rows, :], *_split3(blk)], axis=1)
                ka_ref[rows, :] = jnp.dot(cat, sel_ref[...],
                                          preferred_element_type=F32).astype(BF16)

        pad_row = lax.broadcasted_iota(jnp.int32, (HEAD_DIM, BQ), 0)
        minus_ones = jnp.where(pad_row < N_SPLIT, -1.0, 0.0).astype(BF16)
        for h in range(N_HEADS):
            qm_ref[h] = jnp.concatenate([qt[h * HEAD_DIM:(h + 1) * HEAD_DIM], minus_ones], axis=0)
    else:
        r = lax.broadcasted_iota(jnp.int32, qt.shape, 0)
        n_rot = N_HEADS * ROPE_DIM
        head_of_row = jnp.where(r < n_rot, r // ROPE_DIM, (r - n_rot) // REST_DIM)
        for h in range(N_HEADS):
            qm_ref[h] = jnp.where(head_of_row == h, qt, jnp.zeros_like(qt))
    acc_ref[...] = jnp.zeros_like(acc_ref)
    m_ref[...] = jnp.full(m_ref.shape, NEG_BIG, F32)
    krow = lax.broadcasted_iota(jnp.int32, (BK, BQ), 0)
    qcol = lax.broadcasted_iota(jnp.int32, (BK, BQ), 1)
    tri = (lax.broadcasted_iota(jnp.int32, (BK, BK), 0)
           <= lax.broadcasted_iota(jnp.int32, (BK, BK), 1))
    upper = slice(BK, BQ)

    def fetch(j, last=False):
        start = pl.multiple_of(j * BK, BK)
        if fox:
            kbs = [ka_ref[pl.ds(start, BK), 2 * h * HEAD_DIM:2 * (h + 1) * HEAD_DIM]
                   for h in range(N_HEADS)]
            aux = None
        else:
            kbs = [k_ref[0, pl.ds(start, BK), :]] * N_HEADS
            halves = (1,) if last else range(R)
            aux = jnp.concatenate([aux_ref[R * qi + half - j] for half in halves], axis=1)
        vbs = [vt_ref[0, j, h * VT_ROWS:(h + 1) * VT_ROWS, :] for h in range(N_HEADS)]
        return kbs, vbs, aux

    def head_stage(h, blk, slot, nxt=None, nxt_last=False, diag=False, last=False):
        _, vbs, aux = blk
        cols = upper if last else slice(0, BQ)
        nxt_cols = upper if nxt_last else slice(0, BQ)
        if nxt is not None:
            st_slots[1 - slot][h, :, nxt_cols] = jnp.dot(
                nxt[0][h], qm_ref[h, :, nxt_cols], preferred_element_type=F32)
        st = st_slots[slot][h, :, cols]
        if not fox:
            st = st + aux
        elif last:
            st = jnp.where(tri, st, NEG_BIG)
        elif diag:
            st = jnp.where(krow <= qcol, st, NEG_BIG)
        m_old = m_ref[h, :, cols]
        m_new = jnp.maximum(m_old, jnp.max(st, axis=0, keepdims=True))
        m_ref[h, :, cols] = m_new
        p = jnp.exp2(st - m_new)
        alpha = jnp.exp2(m_old - m_new)
        rows = slice(h * VT_ROWS, (h + 1) * VT_ROWS)
        acc_ref[rows, cols] = alpha * acc_ref[rows, cols] + jnp.dot(
            vbs[h], p.astype(BF16), preferred_element_type=F32)

    def first_scores(h, blk):
        st_slots[0][h] = jnp.dot(blk[0][h], qm_ref[h], preferred_element_type=F32)

    def finish():
        outs = []
        for h in range(N_HEADS):
            num = acc_ref[h * VT_ROWS:h * VT_ROWS + HEAD_DIM, :]
            den = acc_ref[h * VT_ROWS + HEAD_DIM:h * VT_ROWS + HEAD_DIM + 1, :]
            outs.append(num / den)
        o_ref[0] = jnp.concatenate(outs, axis=0).T.astype(BF16)

    return fetch, head_stage, first_scores, finish


def _attn_kernel(fqt_ref, fk_ref, fvt_ref, lf_ref, sel_ref, dqt_ref, dk_ref, dvt_ref, tab_ref,
                 of_ref, od_ref,
                 fqm, facc, fm, fst0, fst1, fka, dqm, dacc, dm, dst0, dst1):
    qi = pl.program_id(1)
    R = BQ // BK
    groups = (
        _attn_pipeline(True, qi, fqt_ref, fk_ref, fvt_ref, lf_ref, sel_ref, of_ref,
                       fqm, facc, fm, (fst0, fst1), fka),
        _attn_pipeline(False, qi, dqt_ref, dk_ref, dvt_ref, tab_ref, None, od_ref,
                       dqm, dacc, dm, (dst0, dst1), None),
    )

    def stage(blks, slot, nxts=None, **flags):
        for h in range(N_HEADS):
            for g, (_, head_stage, _, _) in enumerate(groups):
                head_stage(h, blks[g], slot, nxt=None if nxts is None else nxts[g], **flags)

    def fetch_all(j, **kw):
        return [fetch(j, **kw) for fetch, _, _, _ in groups]

    def pair(i, _):
        b0, b1, b2 = fetch_all(2 * i), fetch_all(2 * i + 1), fetch_all(2 * i + 2)
        stage(b0, 0, b1)
        stage(b1, 1, b2)
        return 0

    first = fetch_all(0)
    for h in range(N_HEADS):
        for g, (_, _, first_scores, _) in enumerate(groups):
            first_scores(h, first[g])
    lax.fori_loop(0, qi, pair, 0)
    b0, b1 = fetch_all(R * qi), fetch_all(R * qi + 1, last=True)
    stage(b0, 0, b1, nxt_last=True, diag=True)
    stage(b1, 1, last=True)
    for _, _, _, finish in groups:
        finish()


def _fox_slot_matrix():
    sel = np.zeros((ATT_W + N_SPLIT * FF_LANES, N_HEADS * 2 * HEAD_DIM), np.float32)
    for h in range(N_HEADS):
        for d in range(HEAD_DIM):
            sel[h * HEAD_DIM + d, 2 * h * HEAD_DIM + d] = 1.0
        for piece in range(N_SPLIT):
            sel[ATT_W + piece * FF_LANES + h, (2 * h + 1) * HEAD_DIM + piece] = 1.0
    return jnp.asarray(sel, BF16)


def _attention(fqt, fk, fvt, lf, dqt, dk, dvt, tab):
    B, T, W = fk.shape
    sel = _fox_slot_matrix()
    q_spec = pl.BlockSpec((1, W, BQ), lambda b, q: (b, 0, q))
    k_spec = pl.BlockSpec((1, T, W), lambda b, q: (b, 0, 0))
    v_spec = pl.BlockSpec((1, T // BK, N_HEADS * VT_ROWS, BK), lambda b, q: (b, 0, 0, 0))
    o_spec = pl.BlockSpec((1, BQ, W), lambda b, q: (b, q, 0))
    o_shape = jax.ShapeDtypeStruct((B, T, W), BF16)

    def group_scratch(q_rows):
        return [pltpu.VMEM((N_HEADS, q_rows, BQ), BF16),
                pltpu.VMEM((N_HEADS * VT_ROWS, BQ), F32),
                pltpu.VMEM((N_HEADS, 1, BQ), F32),
                pltpu.VMEM((N_HEADS, BK, BQ), F32), pltpu.VMEM((N_HEADS, BK, BQ), F32)]

    return pl.pallas_call(
        _attn_kernel,
        grid=(B, T // BQ),
        in_specs=[q_spec, k_spec, v_spec,
                  pl.BlockSpec((1, T, FF_LANES), lambda b, q: (b, 0, 0)),
                  pl.BlockSpec(sel.shape, lambda b, q: (0, 0)),
                  q_spec, k_spec, v_spec,
                  pl.BlockSpec(tab.shape, lambda b, q: (0, 0, 0))],
        out_specs=[o_spec, o_spec],
        out_shape=[o_shape, o_shape],
        scratch_shapes=(group_scratch(2 * HEAD_DIM)
                        + [pltpu.VMEM((T, N_HEADS * 2 * HEAD_DIM), BF16)]
                        + group_scratch(W)),
        compiler_params=_params(("parallel", "arbitrary")),
        name="attention",
    )(fqt, fk, fvt, lf, sel, dqt, dk, dvt, tab)


def _dilated_bias_table():
    nd = max(w for w, _ in DILATED_PATTERNS) // BK
    d = (np.arange(nd)[:, None, None] * BK + np.arange(BK)[None, None, :]
         - np.arange(BK)[None, :, None])
    mult = np.zeros(d.shape, np.float64)
    for window, dil in DILATED_PATTERNS:
        mult += (d >= 0) & (d <= window) & (d % dil == 0)
    tab = np.where(mult > 0, np.log2(np.maximum(mult, 1.0)), NEG_BIG)
    return jnp.asarray(tab, F32)


def _hgrn_kernel(q_ref, lf_ref, k_ref, v_ref, s_ref, nw_ref, o_ref,
                 st_ref, b_scr, k_scr, a_scr, g_scr, bn_scr, ta_scr, ti_scr, tv_scr, tg_scr):
    C, S, E = HG_CHUNK, HG_SUB, HG_E
    NS = C // S
    n_chunks = HG_TB // C
    heads = range(HG_HEADS)
    lanes = [slice(hd * E, (hd + 1) * E) for hd in heads]

    @pl.when(pl.program_id(1) == 0)
    def _():
        st_ref[...] = jnp.zeros_like(st_ref)

    r = lax.broadcasted_iota(jnp.int32, (C, C), 0)
    c = lax.broadcasted_iota(jnp.int32, (C, C), 1)
    tril3 = _tril3(C)
    diag01 = jnp.where(((c // S) == (r // S)) & (c <= r), 1.0, 0.0)
    rr = lax.broadcasted_iota(jnp.int32, (S * E, C), 0)
    rc = lax.broadcasted_iota(jnp.int32, (S * E, C), 1)
    red = jnp.where(rr // E == rc % S, 1.0, 0.0).astype(BF16)
    not_first01 = jnp.where(lax.broadcasted_iota(jnp.int32, (NS, E), 0) == 0, 0.0, 1.0)
    zeros_sub = jnp.zeros((S, E), F32)

    def rows_bcast(ref, hd, idx):
        return jnp.concatenate(
            [jnp.broadcast_to(ref[hd, idx(i):idx(i) + 1, :], (S, E)) for i in range(NS)], axis=0)

    def cumsum(ci):
        r0 = pl.multiple_of(ci * C, C)
        return [_cumsum_rows(tril3, _split3(lf_ref[0, pl.ds(r0, C), lanes[hd]])) for hd in heads]

    def out_dots():
        return [ti_scr[hd] + jnp.dot(ta_scr[hd], tv_scr[hd], preferred_element_type=F32)
                for hd in heads]

    def out_finish(ci, os_, gates, hd):
        r0 = pl.multiple_of(ci * C, C)
        o = os_[hd]
        o = o * lax.rsqrt(jnp.mean(o * o, axis=-1, keepdims=True) + EPS)
        o_ref[0, pl.ds(r0, C), lanes[hd]] = (o * gates[hd]).astype(BF16)

    def scores(ci, bs_, after_head):
        r0 = pl.multiple_of(ci * C, C)
        for hd in heads:
            b = bs_[hd]
            q = q_ref[0, pl.ds(r0, C), lanes[hd]]
            k = k_ref[0, pl.ds(r0, C), lanes[hd]]
            v = v_ref[0, pl.ds(r0, C), lanes[hd]]
            b_last = b[C - 1:C, :]
            st = st_ref[hd]
            qe = (q * jnp.exp2(b)).astype(BF16)
            ti_scr[hd] = lax.dot_general(qe, st.astype(BF16), (((1,), (1,)), ((), ())),
                                         preferred_element_type=F32)
            kdec = (k * jnp.exp2(b_last - b)).astype(BF16)
            st_ref[hd] = st * jnp.exp2(b_last) + jnp.dot(
                v.T.astype(BF16), kdec, preferred_element_type=F32)
            tv_scr[hd] = v.astype(BF16)
            tg_scr[hd] = s_ref[0, pl.ds(r0, C), lanes[hd]] * nw_ref[:, lanes[hd]]
            b_scr[hd] = b
            k_scr[hd] = k
            b_end = b_scr[hd, pl.ds(S - 1, NS, stride=S), :]
            b_prev = pltpu.roll(b_end, 1, axis=0) * not_first01
            a_scr[hd, 0:NS] = b_end
            a_scr[hd, NS:2 * NS] = b_prev
            q_t = q * jnp.exp2(b - rows_bcast(a_scr, hd, lambda i: NS + i))
            k_h = k * jnp.exp2(rows_bcast(a_scr, hd, lambda i: i) - b)
            for j in range(NS - 1):
                g_scr[hd, j] = jnp.exp2(jnp.minimum(b_prev - b_end[j:j + 1, :], 0.0))
            lhs_rows, rhs_rows = [], []
            for i in range(NS):
                qi_t = q_t[i * S:(i + 1) * S, :]
                lhs_rows.append(jnp.concatenate(
                    [qi_t * g_scr[hd, j, i:i + 1, :] if i > j else zeros_sub
                     for j in range(NS - 1)], axis=1))
                rhs_rows.append(jnp.concatenate(
                    [k_h[i * S:(i + 1) * S, :] if i == j else zeros_sub
                     for j in range(NS - 1)], axis=1))
            a_off = lax.dot_general(jnp.concatenate(lhs_rows, axis=0).astype(BF16),
                                    jnp.concatenate(rhs_rows, axis=0).astype(BF16),
                                    (((1,), (1,)), ((), ())),
                                    preferred_element_type=F32)
            w_parts = []
            for s in range(S):
                bs = rows_bcast(b_scr, hd, lambda i: i * S + s)
                ks = rows_bcast(k_scr, hd, lambda i: i * S + s)
                w_parts.append((jnp.exp2(jnp.minimum(b - bs, 0.0)) * (q * ks)).astype(BF16))
            a_diag = jnp.dot(jnp.concatenate(w_parts, axis=1), red,
                             preferred_element_type=F32)
            ta_scr[hd] = (a_off + a_diag * diag01).astype(BF16)
            after_head(hd)

    def step(ci, _):
        bs_ = [bn_scr[hd] for hd in heads]
        gates = [tg_scr[hd] for hd in heads]
        nxt = cumsum(jnp.minimum(ci + 1, n_chunks - 1))
        os_ = out_dots()
        scores(ci, bs_, functools.partial(out_finish, jnp.maximum(ci - 1, 0), os_, gates))
        for hd in heads:
            bn_scr[hd] = nxt[hd]
        return 0

    for ref in (ta_scr, ti_scr, tv_scr, tg_scr):
        ref[...] = jnp.zeros_like(ref)
    first = cumsum(0)
    for hd in heads:
        bn_scr[hd] = first[hd]
    lax.fori_loop(0, n_chunks, step, 0)
    last_os, last_gates = out_dots(), [tg_scr[hd] for hd in heads]
    for hd in heads:
        out_finish(n_chunks - 1, last_os, last_gates, hd)


def _hgrn(hq, hl, hk, hv, hs, layer, nw):
    B, T, W = hq.shape
    row = lambda b, t: (b, t, 0)
    blk = pl.BlockSpec((1, HG_TB, W), row)
    C, NS = HG_CHUNK, HG_CHUNK // HG_SUB
    per_head = lambda *shape, dtype=F32: pltpu.VMEM((HG_HEADS,) + shape, dtype)
    return pl.pallas_call(
        _hgrn_kernel,
        grid=(B, T // HG_TB),
        in_specs=[blk, blk, blk, blk, blk,
                  pl.BlockSpec((None, 1, W), lambda b, t: (layer, 0, 0))],
        out_specs=pl.BlockSpec((1, HG_TB, W), row),
        out_shape=jax.ShapeDtypeStruct((B, T, W), BF16),
        scratch_shapes=[
            per_head(HG_V, HG_E),
            per_head(C, HG_E),
            per_head(C, HG_E),
            per_head(2 * NS, HG_E),
            per_head(NS - 1, NS, HG_E),
            per_head(C, HG_E),
            per_head(C, C, dtype=BF16),
            per_head(C, HG_V),
            per_head(C, HG_V, dtype=BF16),
            per_head(C, HG_V),
        ],
        compiler_params=_params(("parallel", "arbitrary")),
        name="hgrn2",
    )(hq, hl, hk, hv, hs, nw)


W_IN_SIZES = (ATT_W, ATT_W, ATT_W, N_HEADS, ATT_W, ATT_W, ATT_W, HG_W, HG_W, HG_W, HG_W)
W_IN_GROUP = 2 * LANES
COPY_LO, COPY_HI, FORGET, PERM_LO, PERM_HI, ROTARY = range(6)


def _relayout_plan():
    fq, fk, fv, ff, dq, dk, dv, hq, hf, hi, hg = (int(e) for e in np.cumsum((0,) + W_IN_SIZES)[:-1])
    copy = lambda start, width: [(start + o, kind) for o in range(0, width, W_IN_GROUP)
                                 for kind in (COPY_LO, COPY_HI)]
    dilated = lambda start: [(start, PERM_LO), (start, PERM_HI), (start, ROTARY)]
    wa = copy(fk, ATT_W) + dilated(dk) + [(ff, FORGET)] + copy(hq, 4 * HG_W)
    wb = copy(fq, ATT_W) + copy(fv, ATT_W) + dilated(dq) + copy(dv, ATT_W)
    return wa, wb


def _relayout_kernel(start_ref, kind_ref, w_ref, o_ref):
    del start_ref
    kind = kind_ref[pl.program_id(0)]
    half = ROPE_DIM // 2
    row = lax.broadcasted_iota(jnp.int32, (LANES, D_MODEL), 0)
    n_rot = N_HEADS * ROPE_DIM

    def head(x, h, lo, hi):
        return x[h * HEAD_DIM + lo:h * HEAD_DIM + hi]

    def rest(x):
        return jnp.concatenate([head(x, h, ROPE_DIM, HEAD_DIM) for h in range(N_HEADS)], axis=0)

    def perm_lo(x):
        return jnp.concatenate([head(x, h, 0, ROPE_DIM) for h in range(N_HEADS)]
                               + [rest(x)[:LANES - n_rot]], axis=0)

    def rotary(x):
        return jnp.concatenate(
            [part for h in range(N_HEADS)
             for part in (-head(x, h, half, ROPE_DIM), head(x, h, 0, half))]
            + [jnp.zeros((LANES - n_rot, D_MODEL), F32)], axis=0)

    variants = {COPY_LO: lambda x: x[:LANES], COPY_HI: lambda x: x[LANES:],
                FORGET: lambda x: jnp.where(row < N_HEADS, x[:LANES], 0.0),
                PERM_LO: perm_lo, PERM_HI: lambda x: rest(x)[LANES - n_rot:], ROTARY: rotary}
    for k, variant in variants.items():
        @pl.when(kind == k)
        def _(variant=variant):
            for layer in range(o_ref.shape[0]):
                x = w_ref[:, layer, :]
                o_ref[layer] = variant(x).T.astype(BF16)


def _relayout_w_in(w_in):
    depth, d_model, _ = w_in.shape
    w_t = w_in.transpose(2, 0, 1)
    outs = []
    for plan in _relayout_plan():
        starts = jnp.asarray([s for s, _ in plan], jnp.int32)
        kinds = jnp.asarray([k for _, k in plan], jnp.int32)
        outs.append(pl.pallas_call(
            _relayout_kernel,
            grid_spec=pltpu.PrefetchScalarGridSpec(
                num_scalar_prefetch=2,
                grid=(len(plan),),
                in_specs=[pl.BlockSpec(
                    (pl.Element(W_IN_GROUP), pl.Element(depth), pl.Element(d_model)),
                    lambda g, starts, kinds: (starts[g], 0, 0))],
                out_specs=pl.BlockSpec((depth, d_model, LANES), lambda g, starts, kinds: (0, 0, g)),
            ),
            out_shape=jax.ShapeDtypeStruct((depth, d_model, len(plan) * LANES), BF16),
            compiler_params=_params(("arbitrary",)),
            name="w_in_relayout",
        )(starts, kinds, w_t))
    return outs


def _rope_tables(positions):
    half = ROPE_DIM // 2
    freqs = ROPE_THETA ** (-jnp.arange(0, ROPE_DIM, 2, dtype=F32) / ROPE_DIM)
    ang = positions.astype(F32)[:, :, None] * freqs
    lane = np.arange(ROT_LANES)
    is_rot = lane < N_HEADS * ROPE_DIM
    place = ((lane[None, :] % half == np.arange(half)[:, None]) & is_rot[None, :]).astype(np.float32)
    expand = lambda t: jnp.einsum("bth,hl->btl", t, place, precision=lax.Precision.HIGHEST)
    cos = expand(jnp.cos(ang)) + jnp.asarray(~is_rot, F32)
    return cos, expand(jnp.sin(ang))


def kernel(x, positions, ffn1_norm, ffn1_w_gate, ffn1_w_up, ffn1_w_down, mix_norm, w_in,
           fox_forget_bias, hgrn_lower_bounds, hgrn_out_norm, w_out, ffn2_norm,
           ffn2_w_gate, ffn2_w_up, ffn2_w_down, final_norm):
    B, T, D = x.shape
    depth = w_in.shape[0]
    assert D == D_MODEL and T % max(TM_PROJ, HG_TB, BQ) == 0 and (B * T) % TM_FFN == 0
    assert BQ % BK == 0 and TM_PROJ % BK == 0

    sm = jax.nn.softmax(hgrn_lower_bounds.astype(F32), axis=0)
    lbs = (jnp.cumsum(sm, axis=0) - sm[0:1]).reshape(depth, 1, HG_W)
    wa, wb = _relayout_w_in(w_in)
    fb = jnp.pad(fox_forget_bias.astype(F32), ((0, 0), (0, FF_LANES - N_HEADS)))
    fb = fb.reshape(depth, 1, FF_LANES)
    stack_row = lambda p: p.astype(F32).reshape(depth, 1, p.shape[-1])
    n1, nm, n2, nh = (stack_row(p) for p in (ffn1_norm, mix_norm, ffn2_norm, hgrn_out_norm))
    w1 = (ffn1_w_gate, ffn1_w_up, ffn1_w_down)
    w2 = (ffn2_w_gate, ffn2_w_up, ffn2_w_down)
    wo = w_out
    fnorm = final_norm.astype(F32).reshape(1, D)

    cos, sin = _rope_tables(positions)
    dil_tab = _dilated_bias_table()

    xf = x.reshape(B * T, D)
    for i in range(depth):
        xf = _ffn(xf, i, n1, *w1)
        (fk_a, lf_a, dk_a, hq_a, hl_a, hk_a, hv_a, hs_a, fqt_a, fvt_a, dqt_a, dvt_a) = _inproj(
            xf.reshape(B, T, D), i, nm, wa, wb, cos, sin, fb, lbs)
        oa, ob = _attention(fqt_a, fk_a, fvt_a, lf_a, dqt_a, dk_a, dvt_a, dil_tab)
        oc = _hgrn(hq_a, hl_a, hk_a, hv_a, hs_a, i, nh)
        xf = _mix_ffn(xf, oa.reshape(B * T, ATT_W), ob.reshape(B * T, ATT_W),
                      oc.reshape(B * T, HG_W), i, wo, n2, *w2, fnorm, i == depth - 1)
    return xf.reshape(B, T, D)
```

```python
import functools
import math

import numpy as np
import jax
import jax.numpy as jnp
from jax import lax
from jax.experimental import pallas as pl
from jax.experimental.pallas import tpu as pltpu

F32 = jnp.float32
BF16 = jnp.bfloat16

D_MODEL = 1024
HEAD_DIM = 64
N_HEADS = 4
ATT_W = N_HEADS * HEAD_DIM
HG_HEADS = 4
HG_E = 128
HG_V = 128
HG_W = HG_HEADS * HG_V
ROPE_DIM = HEAD_DIM // 4
ROPE_THETA = 500000.0
D_FF = ((8 * D_MODEL // 3 + 127) // 128) * 128
EPS = 1e-6
NEG_BIG = -1e30
LB_FLOOR = 1e-30
DILATED_PATTERNS = ((128, 1), (512, 4), (2048, 16))
FF_LANES = 128
VT_ROWS = HEAD_DIM + 16
LOG2E = 1.0 / math.log(2.0)
QK_SCALE = HEAD_DIM ** -0.5 * LOG2E
N_SPLIT = 3
ROT_LANES = 128
REST_DIM = HEAD_DIM - ROPE_DIM

V7X_VMEM_BYTES = 64 * 1024 * 1024
VMEM_LIMIT = V7X_VMEM_BYTES - 8 * 1024 * 1024
SUBLANES = 8
LANES = 128

TM_FFN = 512
FF_CHUNK = 512
TM_PROJ = 1024
BQ = 512
BK = 256
HG_CHUNK = 64
HG_SUB = SUBLANES
HG_TB = 1024
CS_BLK = 256


def _rms(x, g):
    ms = jnp.mean(x * x, axis=-1, keepdims=True)
    return x * lax.rsqrt(ms + EPS) * g


def _sigmoid(x):
    return 1.0 / (1.0 + jnp.exp(-x))


def _split3(x):
    hi = x.astype(BF16)
    rest = x - hi.astype(F32)
    mid = rest.astype(BF16)
    lo = (rest - mid.astype(F32)).astype(BF16)
    return hi, mid, lo


def _tril3(n):
    width = -(-N_SPLIT * n // LANES) * LANES
    r = lax.broadcasted_iota(jnp.int32, (n, width), 0)
    c = lax.broadcasted_iota(jnp.int32, (n, width), 1)
    return jnp.where((c % n <= r) & (c < N_SPLIT * n), 1.0, 0.0).astype(BF16)


def _cumsum_rows(tril3, pieces):
    pieces = list(pieces)
    rows, cols = pieces[0].shape
    pad = tril3.shape[1] - N_SPLIT * rows
    if pad:
        pieces.append(jnp.zeros((pad, cols), BF16))
    return jnp.dot(tril3, jnp.concatenate(pieces, axis=0), preferred_element_type=F32)


def _params(sem):
    return pltpu.CompilerParams(dimension_semantics=sem, vmem_limit_bytes=VMEM_LIMIT)


def _swiglu_residual(x, g_ref, wg_ref, wu_ref, wd_ref):
    h = _rms(x, g_ref[...]).astype(BF16)
    acc = jnp.zeros_like(x)
    for c in range(0, D_FF, FF_CHUNK):
        w = min(FF_CHUNK, D_FF - c)
        g = jnp.dot(h, wg_ref[:, c:c + w].astype(BF16), preferred_element_type=F32)
        u = jnp.dot(h, wu_ref[:, c:c + w].astype(BF16), preferred_element_type=F32)
        a = (g * _sigmoid(g) * u).astype(BF16)
        acc = acc + jnp.dot(a, wd_ref[c:c + w, :].astype(BF16), preferred_element_type=F32)
    return x + 0.5 * acc


def _ffn_kernel(x_ref, g_ref, wg_ref, wu_ref, wd_ref, o_ref):
    o_ref[...] = _swiglu_residual(x_ref[...], g_ref, wg_ref, wu_ref, wd_ref)


def _mix_ffn_kernel(final, x_ref, oa_ref, ob_ref, oc_ref, wo_ref, g_ref, wg_ref, wu_ref, wd_ref,
                    fg_ref, o_ref):
    mixed = jnp.concatenate([oa_ref[...], ob_ref[...], oc_ref[...]], axis=1)
    x = x_ref[...] + jnp.dot(mixed, wo_ref[...].astype(BF16), preferred_element_type=F32)
    y = _swiglu_residual(x, g_ref, wg_ref, wu_ref, wd_ref)
    if final:
        y = _rms(y, fg_ref[...])
    o_ref[...] = y


def _layer_spec(arr, layer):
    return pl.BlockSpec((None,) + arr.shape[1:], lambda i: (layer, 0, 0),
                        pipeline_mode=pl.Buffered(1))


def _ffn(x2d, layer, g, wg, wu, wd):
    n = x2d.shape[0]
    row = lambda i: (i, 0)
    return pl.pallas_call(
        _ffn_kernel,
        grid=(n // TM_FFN,),
        in_specs=[pl.BlockSpec((TM_FFN, D_MODEL), row), _layer_spec(g, layer),
                  _layer_spec(wg, layer), _layer_spec(wu, layer), _layer_spec(wd, layer)],
        out_specs=pl.BlockSpec((TM_FFN, D_MODEL), row),
        out_shape=jax.ShapeDtypeStruct((n, D_MODEL), F32),
        compiler_params=_params(("parallel",)),
        name="ffn",
    )(x2d, g, wg, wu, wd)


def _mix_ffn(x2d, oa, ob, oc, layer, wo, g, wg, wu, wd, fg, final):
    n = x2d.shape[0]
    row = lambda i: (i, 0)
    return pl.pallas_call(
        functools.partial(_mix_ffn_kernel, final),
        grid=(n // TM_FFN,),
        in_specs=[pl.BlockSpec((TM_FFN, D_MODEL), row),
                  pl.BlockSpec((TM_FFN, ATT_W), row),
                  pl.BlockSpec((TM_FFN, ATT_W), row),
                  pl.BlockSpec((TM_FFN, HG_W), row),
                  _layer_spec(wo, layer), _layer_spec(g, layer),
                  _layer_spec(wg, layer), _layer_spec(wu, layer), _layer_spec(wd, layer),
                  pl.BlockSpec((1, D_MODEL), lambda i: (0, 0))],
        out_specs=pl.BlockSpec((TM_FFN, D_MODEL), row),
        out_shape=jax.ShapeDtypeStruct((n, D_MODEL), F32),
        compiler_params=_params(("parallel",)),
        name="mix_ffn_final" if final else "mix_ffn",
    )(x2d, oa, ob, oc, wo, g, wg, wu, wd, fg)


def _inproj_kernel(x_ref, g_ref, wa_ref, wb_ref, cos_ref, sin_ref, fb_ref, lb_ref,
                   fk_ref, lf_ref, dk_ref, hq_ref, hl_ref, hk_ref, hv_ref, hs_ref,
                   fqt_ref, fvt_ref, dqt_ref, dvt_ref):
    h = _rms(x_ref[0], g_ref[...]).astype(BF16)
    W = ATT_W
    R = ROT_LANES
    base = 2 * W + R + FF_LANES

    def rotary(t, partner):
        return jnp.concatenate([t[:, :R] * cos_ref[0] + partner * sin_ref[0], t[:, R:]], axis=1)

    pa = jnp.dot(h, wa_ref[:, 0:base], preferred_element_type=F32)
    fk_ref[0] = pa[:, 0:W].astype(BF16)
    dk_ref[0] = rotary(pa[:, W:2 * W], pa[:, 2 * W:2 * W + R]).astype(BF16)
    z = pa[:, 2 * W + R:base] + fb_ref[...]
    lf_ref[0] = (jnp.minimum(z, 0.0) - jnp.log1p(jnp.exp(-jnp.abs(z)))) * LOG2E

    hdot = lambda i: jnp.dot(h, wa_ref[:, base + i * HG_W:base + (i + 1) * HG_W],
                             preferred_element_type=F32)
    q = hdot(0)
    hq_ref[0] = q * _sigmoid(q)
    z = hdot(1)
    lbf = jnp.clip(lb_ref[...], 0.0, 1.0 - 1e-6)
    ez = jnp.exp(-jnp.abs(z))
    inv = 1.0 / (1.0 + ez)
    sig_pos = jnp.where(z >= 0, inv, ez * inv)
    sig_neg = jnp.where(z >= 0, ez * inv, inv)
    hl_ref[0] = jnp.log(jnp.maximum(lbf, LB_FLOOR) + (1.0 - lbf) * sig_pos) * LOG2E
    hk_ref[0] = (1.0 - lbf) * sig_neg
    hv_ref[0] = hdot(2)
    hs_ref[0] = _sigmoid(hdot(3))

    pb = jnp.dot(h, wb_ref[...], preferred_element_type=F32)
    fqt_ref[0] = (pb[:, 0:W] * QK_SCALE).T.astype(BF16)
    dq = rotary(pb[:, 2 * W:3 * W], pb[:, 3 * W:3 * W + R]) * QK_SCALE
    dqt_ref[0] = dq.T.astype(BF16)
    ones = jnp.ones((VT_ROWS - HEAD_DIM, BK), BF16)
    for src, ref in ((pb[:, W:2 * W].T.astype(BF16), fvt_ref),
                     (pb[:, 3 * W + R:4 * W + R].T.astype(BF16), dvt_ref)):
        for j in range(TM_PROJ // BK):
            ref[0, j] = jnp.concatenate(
                [part for h in range(N_HEADS)
                 for part in (src[h * HEAD_DIM:(h + 1) * HEAD_DIM, j * BK:(j + 1) * BK], ones)],
                axis=0)


def _inproj(x, layer, g, wa, wb, cos, sin, fb, lb):
    B, T, _ = x.shape
    W = ATT_W
    per_layer = lambda arr: pl.BlockSpec((None,) + arr.shape[1:], lambda b, t: (layer, 0, 0),
                                         pipeline_mode=pl.Buffered(1))
    row = lambda b, t: (b, t, 0)
    col = lambda b, t: (b, 0, t)
    blk4 = lambda b, t: (b, t, 0, 0)
    hg_shape = jax.ShapeDtypeStruct((B, T, HG_W), F32)
    hg_spec = pl.BlockSpec((1, TM_PROJ, HG_W), row)
    out_shape = [
        jax.ShapeDtypeStruct((B, T, W), BF16),
        jax.ShapeDtypeStruct((B, T, FF_LANES), F32),
        jax.ShapeDtypeStruct((B, T, W), BF16),
        hg_shape, hg_shape, hg_shape, hg_shape, hg_shape,
        jax.ShapeDtypeStruct((B, W, T), BF16),
        jax.ShapeDtypeStruct((B, T // BK, N_HEADS * VT_ROWS, BK), BF16),
        jax.ShapeDtypeStruct((B, W, T), BF16),
        jax.ShapeDtypeStruct((B, T // BK, N_HEADS * VT_ROWS, BK), BF16),
    ]
    out_specs = [
        pl.BlockSpec((1, TM_PROJ, W), row),
        pl.BlockSpec((1, TM_PROJ, FF_LANES), row),
        pl.BlockSpec((1, TM_PROJ, W), row),
        hg_spec, hg_spec, hg_spec, hg_spec, hg_spec,
        pl.BlockSpec((1, W, TM_PROJ), col),
        pl.BlockSpec((1, TM_PROJ // BK, N_HEADS * VT_ROWS, BK), blk4),
        pl.BlockSpec((1, W, TM_PROJ), col),
        pl.BlockSpec((1, TM_PROJ // BK, N_HEADS * VT_ROWS, BK), blk4),
    ]
    return pl.pallas_call(
        _inproj_kernel,
        grid=(B, T // TM_PROJ),
        in_specs=[
            pl.BlockSpec((1, TM_PROJ, D_MODEL), row),
            per_layer(g), per_layer(wa), per_layer(wb),
            pl.BlockSpec((1, TM_PROJ, ROT_LANES), row),
            pl.BlockSpec((1, TM_PROJ, ROT_LANES), row),
            per_layer(fb), per_layer(lb),
        ],
        out_specs=out_specs,
        out_shape=out_shape,
        compiler_params=_params(("parallel", "parallel")),
        name="inproj",
    )(x, g, wa, wb, cos, sin, fb, lb)


def _attn_pipeline(fox, qi, qt_ref, k_ref, vt_ref, aux_ref, sel_ref, o_ref,
                   qm_ref, acc_ref, m_ref, st_slots, ka_ref):
    T = k_ref.shape[1]
    R = BQ // BK
    assert R == 2
    qt = qt_ref[0]

    if fox:
        @pl.when(qi == 0)
        def _():
            tril3 = _tril3(CS_BLK)
            blocks = [slice(i * CS_BLK, (i + 1) * CS_BLK) for i in range(T // CS_BLK)]
            local = [_cumsum_rows(tril3, _split3(aux_ref[0, rows, :])) for rows in blocks]
            carry = jnp.zeros((1, FF_LANES), F32)
            for rows, loc in zip(blocks, local):
                blk = loc + carry
                carry = blk[CS_BLK - 1:CS_BLK, :]
                cat = jnp.concatenate([k_ref[0, rows, :], *_split3(blk)], axis=1)
                ka_ref[rows, :] = jnp.dot(cat, sel_ref[...],
                                          preferred_element_type=F32).astype(BF16)

        pad_row = lax.broadcasted_iota(jnp.int32, (HEAD_DIM, BQ), 0)
        minus_ones = jnp.where(pad_row < N_SPLIT, -1.0, 0.0).astype(BF16)
        for h in range(N_HEADS):
            qm_ref[h] = jnp.concatenate([qt[h * HEAD_DIM:(h + 1) * HEAD_DIM], minus_ones], axis=0)
    else:
        r = lax.broadcasted_iota(jnp.int32, qt.shape, 0)
        n_rot = N_HEADS * ROPE_DIM
        head_of_row = jnp.where(r < n_rot, r // ROPE_DIM, (r - n_rot) // REST_DIM)
        for h in range(N_HEADS):
            qm_ref[h] = jnp.where(head_of_row == h, qt, jnp.zeros_like(qt))
    acc_ref[...] = jnp.zeros_like(acc_ref)
    m_ref[...] = jnp.full(m_ref.shape, NEG_BIG, F32)
    krow = lax.broadcasted_iota(jnp.int32, (BK, BQ), 0)
    qcol = lax.broadcasted_iota(jnp.int32, (BK, BQ), 1)
    tri = (lax.broadcasted_iota(jnp.int32, (BK, BK), 0)
           <= lax.broadcasted_iota(jnp.int32, (BK, BK), 1))
    upper = slice(BK, BQ)

    def fetch(j, last=False):
        start = pl.multiple_of(j * BK, BK)
        if fox:
            kbs = [ka_ref[pl.ds(start, BK), 2 * h * HEAD_DIM:2 * (h + 1) * HEAD_DIM]
                   for h in range(N_HEADS)]
            aux = None
        else:
            kbs = [k_ref[0, pl.ds(start, BK), :]] * N_HEADS
            halves = (1,) if last else range(R)
            aux = jnp.concatenate([aux_ref[R * qi + half - j] for half in halves], axis=1)
        vbs = [vt_ref[0, j, h * VT_ROWS:(h + 1) * VT_ROWS, :] for h in range(N_HEADS)]
        return kbs, vbs, aux

    def head_stage(h, blk, slot, nxt=None, nxt_last=False, diag=False, last=False):
        _, vbs, aux = blk
        cols = upper if last else slice(0, BQ)
        nxt_cols = upper if nxt_last else slice(0, BQ)
        if nxt is not None:
            st_slots[1 - slot][h, :, nxt_cols] = jnp.dot(
                nxt[0][h], qm_ref[h, :, nxt_cols], preferred_element_type=F32)
        st = st_slots[slot][h, :, cols]
        if not fox:
            st = st + aux
        elif last:
            st = jnp.where(tri, st, NEG_BIG)
        elif diag:
            st = jnp.where(krow <= qcol, st, NEG_BIG)
        m_old = m_ref[h, :, cols]
        m_new = jnp.maximum(m_old, jnp.max(st, axis=0, keepdims=True))
        m_ref[h, :, cols] = m_new
        p = jnp.exp2(st - m_new)
        alpha = jnp.exp2(m_old - m_new)
        rows = slice(h * VT_ROWS, (h + 1) * VT_ROWS)
        acc_ref[rows, cols] = alpha * acc_ref[rows, cols] + jnp.dot(
            vbs[h], p.astype(BF16), preferred_element_type=F32)

    def first_scores(h, blk):
        st_slots[0][h] = jnp.dot(blk[0][h], qm_ref[h], preferred_element_type=F32)

    def finish():
        outs = []
        for h in range(N_HEADS):
            num = acc_ref[h * VT_ROWS:h * VT_ROWS + HEAD_DIM, :]
            den = acc_ref[h * VT_ROWS + HEAD_DIM:h * VT_ROWS + HEAD_DIM + 1, :]
            outs.append(num / den)
        o_ref[0] = jnp.concatenate(outs, axis=0).T.astype(BF16)

    return fetch, head_stage, first_scores, finish


def _attn_kernel(fqt_ref, fk_ref, fvt_ref, lf_ref, sel_ref, dqt_ref, dk_ref, dvt_ref, tab_ref,
                 of_ref, od_ref,
                 fqm, facc, fm, fst0, fst1, fka, dqm, dacc, dm, dst0, dst1):
    qi = pl.program_id(1)
    R = BQ // BK
    groups = (
        _attn_pipeline(True, qi, fqt_ref, fk_ref, fvt_ref, lf_ref, sel_ref, of_ref,
                       fqm, facc, fm, (fst0, fst1), fka),
        _attn_pipeline(False, qi, dqt_ref, dk_ref, dvt_ref, tab_ref, None, od_ref,
                       dqm, dacc, dm, (dst0, dst1), None),
    )

    def stage(blks, slot, nxts=None, **flags):
        for h in range(N_HEADS):
            for g, (_, head_stage, _, _) in enumerate(groups):
                head_stage(h, blks[g], slot, nxt=None if nxts is None else nxts[g], **flags)

    def fetch_all(j, **kw):
        return [fetch(j, **kw) for fetch, _, _, _ in groups]

    def pair(i, _):
        b0, b1, b2 = fetch_all(2 * i), fetch_all(2 * i + 1), fetch_all(2 * i + 2)
        stage(b0, 0, b1)
        stage(b1, 1, b2)
        return 0

    first = fetch_all(0)
    for h in range(N_HEADS):
        for g, (_, _, first_scores, _) in enumerate(groups):
            first_scores(h, first[g])
    lax.fori_loop(0, qi, pair, 0)
    b0, b1 = fetch_all(R * qi), fetch_all(R * qi + 1, last=True)
    stage(b0, 0, b1, nxt_last=True, diag=True)
    stage(b1, 1, last=True)
    for _, _, _, finish in groups:
        finish()


def _fox_slot_matrix():
    sel = np.zeros((ATT_W + N_SPLIT * FF_LANES, N_HEADS * 2 * HEAD_DIM), np.float32)
    for h in range(N_HEADS):
        for d in range(HEAD_DIM):
            sel[h * HEAD_DIM + d, 2 * h * HEAD_DIM + d] = 1.0
        for piece in range(N_SPLIT):
            sel[ATT_W + piece * FF_LANES + h, (2 * h + 1) * HEAD_DIM + piece] = 1.0
    return jnp.asarray(sel, BF16)


def _attention(fqt, fk, fvt, lf, dqt, dk, dvt, tab):
    B, T, W = fk.shape
    sel = _fox_slot_matrix()
    q_spec = pl.BlockSpec((1, W, BQ), lambda b, q: (b, 0, q))
    k_spec = pl.BlockSpec((1, T, W), lambda b, q: (b, 0, 0))
    v_spec = pl.BlockSpec((1, T // BK, N_HEADS * VT_ROWS, BK), lambda b, q: (b, 0, 0, 0))
    o_spec = pl.BlockSpec((1, BQ, W), lambda b, q: (b, q, 0))
    o_shape = jax.ShapeDtypeStruct((B, T, W), BF16)

    def group_scratch(q_rows):
        return [pltpu.VMEM((N_HEADS, q_rows, BQ), BF16),
                pltpu.VMEM((N_HEADS * VT_ROWS, BQ), F32),
                pltpu.VMEM((N_HEADS, 1, BQ), F32),
                pltpu.VMEM((N_HEADS, BK, BQ), F32), pltpu.VMEM((N_HEADS, BK, BQ), F32)]

    return pl.pallas_call(
        _attn_kernel,
        grid=(B, T // BQ),
        in_specs=[q_spec, k_spec, v_spec,
                  pl.BlockSpec((1, T, FF_LANES), lambda b, q: (b, 0, 0)),
                  pl.BlockSpec(sel.shape, lambda b, q: (0, 0)),
                  q_spec, k_spec, v_spec,
                  pl.BlockSpec(tab.shape, lambda b, q: (0, 0, 0))],
        out_specs=[o_spec, o_spec],
        out_shape=[o_shape, o_shape],
        scratch_shapes=(group_scratch(2 * HEAD_DIM)
                        + [pltpu.VMEM((T, N_HEADS * 2 * HEAD_DIM), BF16)]
                        + group_scratch(W)),
        compiler_params=_params(("parallel", "arbitrary")),
        name="attention",
    )(fqt, fk, fvt, lf, sel, dqt, dk, dvt, tab)


def _dilated_bias_table():
    nd = max(w for w, _ in DILATED_PATTERNS) // BK
    d = (np.arange(nd)[:, None, None] * BK + np.arange(BK)[None, None, :]
         - np.arange(BK)[None, :, None])
    mult = np.zeros(d.shape, np.float64)
    for window, dil in DILATED_PATTERNS:
        mult += (d >= 0) & (d <= window) & (d % dil == 0)
    tab = np.where(mult > 0, np.log2(np.maximum(mult, 1.0)), NEG_BIG)
    return jnp.asarray(tab, F32)


def _hgrn_kernel(q_ref, lf_ref, k_ref, v_ref, s_ref, nw_ref, o_ref,
                 st_ref, b_scr, k_scr, a_scr, g_scr, bn_scr, ta_scr, ti_scr, tv_scr, tg_scr):
    C, S, E = HG_CHUNK, HG_SUB, HG_E
    NS = C // S
    n_chunks = HG_TB // C
    heads = range(HG_HEADS)
    lanes = [slice(hd * E, (hd + 1) * E) for hd in heads]

    @pl.when(pl.program_id(1) == 0)
    def _():
        st_ref[...] = jnp.zeros_like(st_ref)

    r = lax.broadcasted_iota(jnp.int32, (C, C), 0)
    c = lax.broadcasted_iota(jnp.int32, (C, C), 1)
    tril3 = _tril3(C)
    diag01 = jnp.where(((c // S) == (r // S)) & (c <= r), 1.0, 0.0)
    rr = lax.broadcasted_iota(jnp.int32, (S * E, C), 0)
    rc = lax.broadcasted_iota(jnp.int32, (S * E, C), 1)
    red = jnp.where(rr // E == rc % S, 1.0, 0.0).astype(BF16)
    not_first01 = jnp.where(lax.broadcasted_iota(jnp.int32, (NS, E), 0) == 0, 0.0, 1.0)
    zeros_sub = jnp.zeros((S, E), F32)

    def rows_bcast(ref, hd, idx):
        return jnp.concatenate(
            [jnp.broadcast_to(ref[hd, idx(i):idx(i) + 1, :], (S, E)) for i in range(NS)], axis=0)

    def cumsum(ci):
        r0 = pl.multiple_of(ci * C, C)
        return [_cumsum_rows(tril3, _split3(lf_ref[0, pl.ds(r0, C), lanes[hd]])) for hd in heads]

    def out_dots():
        return [ti_scr[hd] + jnp.dot(ta_scr[hd], tv_scr[hd], preferred_element_type=F32)
                for hd in heads]

    def out_finish(ci, os_, gates, hd):
        r0 = pl.multiple_of(ci * C, C)
        o = os_[hd]
        o = o * lax.rsqrt(jnp.mean(o * o, axis=-1, keepdims=True) + EPS)
        o_ref[0, pl.ds(r0, C), lanes[hd]] = (o * gates[hd]).astype(BF16)

    def scores(ci, bs_, after_head):
        r0 = pl.multiple_of(ci * C, C)
        for hd in heads:
            b = bs_[hd]
            q = q_ref[0, pl.ds(r0, C), lanes[hd]]
            k = k_ref[0, pl.ds(r0, C), lanes[hd]]
            v = v_ref[0, pl.ds(r0, C), lanes[hd]]
            b_last = b[C - 1:C, :]
            st = st_ref[hd]
            qe = (q * jnp.exp2(b)).astype(BF16)
            ti_scr[hd] = lax.dot_general(qe, st.astype(BF16), (((1,), (1,)), ((), ())),
                                         preferred_element_type=F32)
            kdec = (k * jnp.exp2(b_last - b)).astype(BF16)
            st_ref[hd] = st * jnp.exp2(b_last) + jnp.dot(
                v.T.astype(BF16), kdec, preferred_element_type=F32)
            tv_scr[hd] = v.astype(BF16)
            tg_scr[hd] = s_ref[0, pl.ds(r0, C), lanes[hd]] * nw_ref[:, lanes[hd]]
            b_scr[hd] = b
            k_scr[hd] = k
            b_end = b_scr[hd, pl.ds(S - 1, NS, stride=S), :]
            b_prev = pltpu.roll(b_end, 1, axis=0) * not_first01
            a_scr[hd, 0:NS] = b_end
            a_scr[hd, NS:2 * NS] = b_prev
            q_t = q * jnp.exp2(b - rows_bcast(a_scr, hd, lambda i: NS + i))
            k_h = k * jnp.exp2(rows_bcast(a_scr, hd, lambda i: i) - b)
            for j in range(NS - 1):
                g_scr[hd, j] = jnp.exp2(jnp.minimum(b_prev - b_end[j:j + 1, :], 0.0))
            lhs_rows, rhs_rows = [], []
            for i in range(NS):
                qi_t = q_t[i * S:(i + 1) * S, :]
                lhs_rows.append(jnp.concatenate(
                    [qi_t * g_scr[hd, j, i:i + 1, :] if i > j else zeros_sub
                     for j in range(NS - 1)], axis=1))
                rhs_rows.append(jnp.concatenate(
                    [k_h[i * S:(i + 1) * S, :] if i == j else zeros_sub
                     for j in range(NS - 1)], axis=1))
            a_off = lax.dot_general(jnp.concatenate(lhs_rows, axis=0).astype(BF16),
                                    jnp.concatenate(rhs_rows, axis=0).astype(BF16),
                                    (((1,), (1,)), ((), ())),
                                    preferred_element_type=F32)
            w_parts = []
            for s in range(S):
                bs = rows_bcast(b_scr, hd, lambda i: i * S + s)
                ks = rows_bcast(k_scr, hd, lambda i: i * S + s)
                w_parts.append((jnp.exp2(jnp.minimum(b - bs, 0.0)) * (q * ks)).astype(BF16))
            a_diag = jnp.dot(jnp.concatenate(w_parts, axis=1), red,
                             preferred_element_type=F32)
            ta_scr[hd] = (a_off + a_diag * diag01).astype(BF16)
            after_head(hd)

    def step(ci, _):
        bs_ = [bn_scr[hd] for hd in heads]
        gates = [tg_scr[hd] for hd in heads]
        nxt = cumsum(jnp.minimum(ci + 1, n_chunks - 1))
        os_ = out_dots()
        scores(ci, bs_, functools.partial(out_finish, jnp.maximum(ci - 1, 0), os_, gates))
        for hd in heads:
            bn_scr[hd] = nxt[hd]
        return 0

    for ref in (ta_scr, ti_scr, tv_scr, tg_scr):
        ref[...] = jnp.zeros_like(ref)
    first = cumsum(0)
    for hd in heads:
        bn_scr[hd] = first[hd]
    lax.fori_loop(0, n_chunks, step, 0)
    last_os, last_gates = out_dots(), [tg_scr[hd] for hd in heads]
    for hd in heads:
        out_finish(n_chunks - 1, last_os, last_gates, hd)


def _hgrn(hq, hl, hk, hv, hs, layer, nw):
    B, T, W = hq.shape
    row = lambda b, t: (b, t, 0)
    blk = pl.BlockSpec((1, HG_TB, W), row)
    C, NS = HG_CHUNK, HG_CHUNK // HG_SUB
    per_head = lambda *shape, dtype=F32: pltpu.VMEM((HG_HEADS,) + shape, dtype)
    return pl.pallas_call(
        _hgrn_kernel,
        grid=(B, T // HG_TB),
        in_specs=[blk, blk, blk, blk, blk,
                  pl.BlockSpec((None, 1, W), lambda b, t: (layer, 0, 0))],
        out_specs=pl.BlockSpec((1, HG_TB, W), row),
        out_shape=jax.ShapeDtypeStruct((B, T, W), BF16),
        scratch_shapes=[
            per_head(HG_V, HG_E),
            per_head(C, HG_E),
            per_head(C, HG_E),
            per_head(2 * NS, HG_E),
            per_head(NS - 1, NS, HG_E),
            per_head(C, HG_E),
            per_head(C, C, dtype=BF16),
            per_head(C, HG_V),
            per_head(C, HG_V, dtype=BF16),
            per_head(C, HG_V),
        ],
        compiler_params=_params(("parallel", "arbitrary")),
        name="hgrn2",
    )(hq, hl, hk, hv, hs, nw)


W_IN_SIZES = (ATT_W, ATT_W, ATT_W, N_HEADS, ATT_W, ATT_W, ATT_W, HG_W, HG_W, HG_W, HG_W)
W_IN_GROUP = 2 * LANES
COPY_LO, COPY_HI, FORGET, PERM_LO, PERM_HI, ROTARY = range(6)


def _relayout_plan():
    fq, fk, fv, ff, dq, dk, dv, hq, hf, hi, hg = (int(e) for e in np.cumsum((0,) + W_IN_SIZES)[:-1])
    copy = lambda start, width: [(start + o, kind) for o in range(0, width, W_IN_GROUP)
                                 for kind in (COPY_LO, COPY_HI)]
    dilated = lambda start: [(start, PERM_LO), (start, PERM_HI), (start, ROTARY)]
    wa = copy(fk, ATT_W) + dilated(dk) + [(ff, FORGET)] + copy(hq, 4 * HG_W)
    wb = copy(fq, ATT_W) + copy(fv, ATT_W) + dilated(dq) + copy(dv, ATT_W)
    return wa, wb


def _relayout_kernel(start_ref, kind_ref, w_ref, o_ref):
    del start_ref
    kind = kind_ref[pl.program_id(0)]
    half = ROPE_DIM // 2
    row = lax.broadcasted_iota(jnp.int32, (LANES, D_MODEL), 0)
    n_rot = N_HEADS * ROPE_DIM

    def head(x, h, lo, hi):
        return x[h * HEAD_DIM + lo:h * HEAD_DIM + hi]

    def rest(x):
        return jnp.concatenate([head(x, h, ROPE_DIM, HEAD_DIM) for h in range(N_HEADS)], axis=0)

    def perm_lo(x):
        return jnp.concatenate([head(x, h, 0, ROPE_DIM) for h in range(N_HEADS)]
                               + [rest(x)[:LANES - n_rot]], axis=0)

    def rotary(x):
        return jnp.concatenate(
            [part for h in range(N_HEADS)
             for part in (-head(x, h, half, ROPE_DIM), head(x, h, 0, half))]
            + [jnp.zeros((LANES - n_rot, D_MODEL), F32)], axis=0)

    variants = {COPY_LO: lambda x: x[:LANES], COPY_HI: lambda x: x[LANES:],
                FORGET: lambda x: jnp.where(row < N_HEADS, x[:LANES], 0.0),
                PERM_LO: perm_lo, PERM_HI: lambda x: rest(x)[LANES - n_rot:], ROTARY: rotary}
    for k, variant in variants.items():
        @pl.when(kind == k)
        def _(variant=variant):
            for layer in range(o_ref.shape[0]):
                x = w_ref[:, layer, :]
                o_ref[layer] = variant(x).T.astype(BF16)


def _relayout_w_in(w_in):
    depth, d_model, _ = w_in.shape
    w_t = w_in.transpose(2, 0, 1)
    outs = []
    for plan in _relayout_plan():
        starts = jnp.asarray([s for s, _ in plan], jnp.int32)
        kinds = jnp.asarray([k for _, k in plan], jnp.int32)
        outs.append(pl.pallas_call(
            _relayout_kernel,
            grid_spec=pltpu.PrefetchScalarGridSpec(
                num_scalar_prefetch=2,
                grid=(len(plan),),
                in_specs=[pl.BlockSpec(
                    (pl.Element(W_IN_GROUP), pl.Element(depth), pl.Element(d_model)),
                    lambda g, starts, kinds: (starts[g], 0, 0))],
                out_specs=pl.BlockSpec((depth, d_model, LANES), lambda g, starts, kinds: (0, 0, g)),
            ),
            out_shape=jax.ShapeDtypeStruct((depth, d_model, len(plan) * LANES), BF16),
            compiler_params=_params(("arbitrary",)),
            name="w_in_relayout",
        )(starts, kinds, w_t))
    return outs


def _rope_tables(positions):
    half = ROPE_DIM // 2
    freqs = ROPE_THETA ** (-jnp.arange(0, ROPE_DIM, 2, dtype=F32) / ROPE_DIM)
    ang = positions.astype(F32)[:, :, None] * freqs
    lane = np.arange(ROT_LANES)
    is_rot = lane < N_HEADS * ROPE_DIM
    place = ((lane[None, :] % half == np.arange(half)[:, None]) & is_rot[None, :]).astype(np.float32)
    expand = lambda t: jnp.einsum("bth,hl->btl", t, place, precision=lax.Precision.HIGHEST)
    cos = expand(jnp.cos(ang)) + jnp.asarray(~is_rot, F32)
    return cos, expand(jnp.sin(ang))


def kernel(x, positions, ffn1_norm, ffn1_w_gate, ffn1_w_up, ffn1_w_down, mix_norm, w_in,
           fox_forget_bias, hgrn_lower_bounds, hgrn_out_norm, w_out, ffn2_norm,
           ffn2_w_gate, ffn2_w_up, ffn2_w_down, final_norm):
    B, T, D = x.shape
    depth = w_in.shape[0]
    assert D == D_MODEL and T % max(TM_PROJ, HG_TB, BQ) == 0 and (B * T) % TM_FFN == 0
    assert BQ % BK == 0 and TM_PROJ % BK == 0

    sm = jax.nn.softmax(hgrn_lower_bounds.astype(F32), axis=0)
    lbs = (jnp.cumsum(sm, axis=0) - sm[0:1]).reshape(depth, 1, HG_W)
    wa, wb = _relayout_w_in(w_in)
    fb = jnp.pad(fox_forget_bias.astype(F32), ((0, 0), (0, FF_LANES - N_HEADS)))
    fb = fb.reshape(depth, 1, FF_LANES)
    stack_row = lambda p: p.astype(F32).reshape(depth, 1, p.shape[-1])
    n1, nm, n2, nh = (stack_row(p) for p in (ffn1_norm, mix_norm, ffn2_norm, hgrn_out_norm))
    w1 = (ffn1_w_gate, ffn1_w_up, ffn1_w_down)
    w2 = (ffn2_w_gate, ffn2_w_up, ffn2_w_down)
    wo = w_out
    fnorm = final_norm.astype(F32).reshape(1, D)

    cos, sin = _rope_tables(positions)
    dil_tab = _dilated_bias_table()

    xf = x.reshape(B * T, D)
    for i in range(depth):
        xf = _ffn(xf, i, n1, *w1)
        (fk_a, lf_a, dk_a, hq_a, hl_a, hk_a, hv_a, hs_a, fqt_a, fvt_a, dqt_a, dvt_a) = _inproj(
            xf.reshape(B, T, D), i, nm, wa, wb, cos, sin, fb, lbs)
        oa, ob = _attention(fqt_a, fk_a, fvt_a, lf_a, dqt_a, dk_a, dvt_a, dil_tab)
        oc = _hgrn(hq_a, hl_a, hk_a, hv_a, hs_a, i, nh)
        xf = _mix_ffn(xf, oa.reshape(B * T, ATT_W), ob.reshape(B * T, ATT_W),
                      oc.reshape(B * T, HG_W), i, wo, n2, *w2, fnorm, i == depth - 1)
    return xf.reshape(B, T, D)
```
